```python
import jax
import jax.numpy as jnp
from jax import lax
import numpy as np

D_MODEL = 2048
BATCH = 4
SEQ = 2048
DEPTH = 2

MIX_WIDTH = D_MODEL
RWKV_WIDTH = MIX_WIDTH // 2
RWKV_HEAD_DIM = 64
RWKV_HEADS = RWKV_WIDTH // RWKV_HEAD_DIM
DECAY_LORA = 64
A_LORA = 64
GATE_LORA = 160
VRES_LORA = 32
MOBA_WIDTH = MIX_WIDTH - RWKV_WIDTH
MOBA_HEAD_DIM = 128
MOBA_HEADS = MOBA_WIDTH // MOBA_HEAD_DIM
MOBA_BLOCK = 256
MOBA_TOPK = 3
MOBA_Q_CHUNK = 16
D_FF = 5632
N_EXPERTS = 8
TOP_K = 2
D_FF_EXPERT = 5632
RMS_EPS = 1e-6
GN_EPS = 64e-5

kernel_name = "hybrid_rwkv7_moba_moe_adaln"


def rwkv_cols(value_residual):
    return 3 * RWKV_WIDTH + DECAY_LORA + A_LORA + GATE_LORA + (VRES_LORA if value_residual else 0)


def rms_norm(x, gain):
    xf = x.astype(jnp.float32)
    y = xf * lax.rsqrt(jnp.mean(xf * xf, axis=-1, keepdims=True) + RMS_EPS)
    return (y * gain.astype(jnp.float32)).astype(x.dtype)


def modulate(h, shift, scale):
    return h * (1 + scale[:, None, :]) + shift[:, None, :]


def token_shift(p, mu):
    prev = jnp.concatenate([jnp.zeros_like(p[:, :1]), p[:, :-1]], axis=1)
    return p + (prev - p) * mu


def rwkv7_recurrence(r, decay, k, v, a, b):
    B, T, H, N = r.shape

    def step(S, inp):
        r_t, w_t, k_t, v_t, a_t, b_t = inp
        sa = jnp.einsum('bhij,bhj->bhi', S, a_t)
        S = S * w_t[:, :, None, :] + sa[..., None] * b_t[:, :, None, :] + v_t[..., None] * k_t[:, :, None, :]
        return S, jnp.einsum('bhij,bhj->bhi', S, r_t)

    S0 = jnp.zeros((B, H, N, N), jnp.float32)
    xs = tuple(jnp.moveaxis(t.astype(jnp.float32), 1, 0) for t in (r, decay, k, v, a, b))
    _, ys = lax.scan(step, S0, xs)
    return jnp.moveaxis(ys, 0, 1)


def rwkv7_mixer(p, v_first, shift_mu, w0, w2, a0, a2, g2, v0, v2, k_k, k_a, r_k, ln_w, ln_b):
    B, T, _ = p.shape
    C, H, N = RWKV_WIDTH, RWKV_HEADS, RWKV_HEAD_DIM
    z = token_shift(p, shift_mu)
    sizes = [C, C, C, DECAY_LORA, A_LORA, GATE_LORA] + ([VRES_LORA] if v0 is not None else [])
    parts = jnp.split(z, np.cumsum(sizes)[:-1].tolist(), axis=-1)
    r, k, v, zw, za, zg = parts[:6]
    w = -jax.nn.softplus(-(w0 + jnp.tanh(zw) @ w2)) - 0.5
    decay = jnp.exp(-jnp.exp(w.astype(jnp.float32)))
    a = jax.nn.sigmoid(a0 + za @ a2)
    g = jax.nn.sigmoid(zg) @ g2
    if v0 is None:
        v_first = v
    else:
        v = v + (v_first - v) * jax.nn.sigmoid(v0 + parts[6] @ v2)
    heads = lambda t: t.reshape(B, T, H, N)
    kk = heads(k * k_k).astype(jnp.float32)
    kk = kk * lax.rsqrt(jnp.maximum(jnp.sum(kk * kk, axis=-1, keepdims=True), 1e-24))
    k = k * (1 + (a - 1) * k_a)
    a_h = heads(a).astype(jnp.float32)
    y = rwkv7_recurrence(heads(r), heads(decay), heads(k), heads(v), -kk, kk * a_h)
    mean = jnp.mean(y, axis=-1, keepdims=True)
    var = jnp.mean(jnp.square(y - mean), axis=-1, keepdims=True)
    y = ((y - mean) * lax.rsqrt(var + GN_EPS)).reshape(B, T, C) * ln_w + ln_b
    bonus = jnp.sum(heads(r) * heads(k) * r_k, axis=-1, keepdims=True) * heads(v)
    out = (y + bonus.reshape(B, T, C).astype(jnp.float32)) * g
    return out.astype(p.dtype), v_first


def moba_attention(q, k, v, gain):
    B, T, _ = q.shape
    H, Dh, BS, QC = MOBA_HEADS, MOBA_HEAD_DIM, MOBA_BLOCK, MOBA_Q_CHUNK
    to_heads = lambda t: t.reshape(B, T, H, Dh).transpose(0, 2, 1, 3)
    q, k, v = to_heads(q), to_heads(k), to_heads(v)
    n_blocks = -(-T // BS)
    pad = n_blocks * BS - T
    kp = jnp.pad(k, ((0, 0), (0, 0), (0, pad), (0, 0)))
    vp = jnp.pad(v, ((0, 0), (0, 0), (0, pad), (0, 0)))
    k_blocks = kp.reshape(B, H, n_blocks, BS, Dh)
    v_blocks = vp.reshape(B, H, n_blocks, BS, Dh)
    k_mean = jnp.mean(k_blocks.astype(jnp.float32), axis=3)
    topk = min(MOBA_TOPK, n_blocks)
    slopes = jnp.exp2(-8.0 * jnp.arange(1, H + 1, dtype=jnp.float32) / H)
    scale = Dh ** -0.5
    b_idx = jnp.arange(B)[:, None, None, None]
    h_idx = jnp.arange(H)[None, :, None, None]
    block_ids = jnp.arange(n_blocks)
    key_offsets = jnp.arange(BS)

    def chunk(ci):
        start = ci * QC
        own = start // BS
        qc = lax.dynamic_slice_in_dim(q, start, QC, axis=2)
        t_pos = (start + jnp.arange(QC))[:, None, None]
        gate = jnp.einsum('bhqd,bhnd->bhqn', qc.astype(jnp.float32), k_mean)
        gate = jnp.where(block_ids < own, gate, -jnp.inf)
        _, sel = lax.top_k(gate, topk)
        valid = (jnp.arange(topk) < own)[:, None]
        k_sel = k_blocks[b_idx, h_idx, sel]
        v_sel = v_blocks[b_idx, h_idx, sel]
        dist_sel = (t_pos - (sel[..., None] * BS + key_offsets)).astype(jnp.float32)
        s_sel = (jnp.einsum('bhqd,bhqksd->bhqks', qc, k_sel).astype(jnp.float32) * scale
                 - slopes[:, None, None, None] * dist_sel)
        s_sel = jnp.where(valid, s_sel, -jnp.inf).reshape(B, H, QC, topk * BS)
        k_own = lax.dynamic_slice_in_dim(kp, own * BS, BS, axis=2)
        v_own = lax.dynamic_slice_in_dim(vp, own * BS, BS, axis=2)
        dist_own = (t_pos[..., 0] - (own * BS + key_offsets)).astype(jnp.float32)
        s_own = (jnp.einsum('bhqd,bhsd->bhqs', qc, k_own).astype(jnp.float32) * scale
                 - slopes[:, None, None] * dist_own)
        s_own = jnp.where(dist_own >= 0, s_own, -jnp.inf)
        probs = jax.nn.softmax(jnp.concatenate([s_sel, s_own], axis=-1), axis=-1).astype(v.dtype)
        p_sel = probs[..., :topk * BS].reshape(B, H, QC, topk, BS)
        p_own = probs[..., topk * BS:]
        return (jnp.einsum('bhqks,bhqksd->bhqd', p_sel, v_sel)
                + jnp.einsum('bhqs,bhsd->bhqd', p_own, v_own))

    out = lax.map(chunk, jnp.arange(T // QC))
    out = out.transpose(1, 0, 3, 2, 4).reshape(B, T, H, Dh)
    out = rms_norm(out, gain.reshape(H, Dh))
    return out.reshape(B, T, MOBA_WIDTH)


def mixer_sublayer(h, v_first, p):
    n_r = p["shift_mu"].shape[0]
    proj = h @ p["w_in"]
    y_r, v_first = rwkv7_mixer(proj[..., :n_r], v_first, p["shift_mu"], p["w0"], p["w2"], p["a0"],
                               p["a2"], p["g2"], p.get("v0"), p.get("v2"), p["k_k"], p["k_a"],
                               p["r_k"], p["ln_w"], p["ln_b"])
    pm = proj[..., n_r:]
    y_m = moba_attention(pm[..., :MOBA_WIDTH], pm[..., MOBA_WIDTH:2 * MOBA_WIDTH],
                         pm[..., 2 * MOBA_WIDTH:], p["moba_gain"])
    return jnp.concatenate([y_r, y_m], axis=-1) @ p["w_out"], v_first


def swiglu(h, w_gate, w_up, w_down):
    return (jax.nn.silu(h @ w_gate) * (h @ w_up)) @ w_down


def moe_swiglu(h, router, w_gate, w_up, w_down):
    logits = (h @ router).astype(jnp.float32)
    top_val, top_idx = lax.top_k(logits, TOP_K)
    top_w = jax.nn.softmax(top_val, axis=-1)
    gates = jnp.sum(jax.nn.one_hot(top_idx, N_EXPERTS, dtype=jnp.float32) * top_w[..., None], axis=-2)
    out = jnp.zeros_like(h)
    for e in range(N_EXPERTS):
        out = out + gates[..., e:e + 1].astype(h.dtype) * swiglu(h, w_gate[e], w_up[e], w_down[e])
    return out


def setup_inputs(seed: int = 0) -> dict:
    key = jax.random.key(seed)
    keys = iter(jax.random.split(key, 96))
    normal = lambda shape, s: s * jax.random.normal(next(keys), shape, jnp.float32)
    uniform = lambda shape, lo, hi: jax.random.uniform(next(keys), shape, jnp.float32, lo, hi)
    D, C, H, N = D_MODEL, RWKV_WIDTH, RWKV_HEADS, RWKV_HEAD_DIM
    inp = {}
    inp["x"] = normal((BATCH, SEQ, D), 1.0)
    inp["c"] = normal((BATCH, D), 1.0)
    for i in range(DEPTH):
        pre = "l%d_" % i
        vres = i > 0
        inp[pre + "mod_w"] = normal((D, 6 * D), 0.5 * D ** -0.5)
        inp[pre + "mod_b"] = normal((6 * D,), 0.02)
        inp[pre + "norm_mix"] = 1.0 + normal((D,), 0.02)
        inp[pre + "w_in"] = normal((D, rwkv_cols(vres) + 3 * MOBA_WIDTH), D ** -0.5)
        inp[pre + "shift_mu"] = uniform((rwkv_cols(vres),), 0.0, 1.0)
        inp[pre + "w0"] = uniform((C,), -6.5, -1.5)
        inp[pre + "w2"] = normal((DECAY_LORA, C), 0.5 * DECAY_LORA ** -0.5)
        inp[pre + "a0"] = normal((C,), 0.1)
        inp[pre + "a2"] = normal((A_LORA, C), 0.5 * A_LORA ** -0.5)
        inp[pre + "g2"] = normal((GATE_LORA, C), GATE_LORA ** -0.5)
        if vres:
            inp[pre + "v0"] = normal((C,), 0.1)
            inp[pre + "v2"] = normal((VRES_LORA, C), 0.5 * VRES_LORA ** -0.5)
        inp[pre + "k_k"] = 0.85 + normal((C,), 0.02)
        inp[pre + "k_a"] = 1.0 + normal((C,), 0.02)
        inp[pre + "r_k"] = normal((H, N), 0.1)
        inp[pre + "ln_w"] = 1.0 + normal((C,), 0.02)
        inp[pre + "ln_b"] = normal((C,), 0.02)
        inp[pre + "moba_gain"] = 1.0 + normal((MOBA_WIDTH,), 0.02)
        inp[pre + "w_out"] = normal((MIX_WIDTH, D), MIX_WIDTH ** -0.5)
        inp[pre + "norm_ffn"] = 1.0 + normal((D,), 0.02)
        if i % 2 == 0:
            inp[pre + "ffn_gate"] = normal((D, D_FF), D ** -0.5)
            inp[pre + "ffn_up"] = normal((D, D_FF), D ** -0.5)
            inp[pre + "ffn_down"] = normal((D_FF, D), D_FF ** -0.5)
        else:
            inp[pre + "router"] = normal((D, N_EXPERTS), D ** -0.5)
            inp[pre + "exp_gate"] = normal((N_EXPERTS, D, D_FF_EXPERT), D ** -0.5)
            inp[pre + "exp_up"] = normal((N_EXPERTS, D, D_FF_EXPERT), D ** -0.5)
            inp[pre + "exp_down"] = normal((N_EXPERTS, D_FF_EXPERT, D), D_FF_EXPERT ** -0.5)
    inp["norm_out"] = 1.0 + normal((D,), 0.02)
    return inp


def reference(x, c,
              l0_mod_w, l0_mod_b, l0_norm_mix, l0_w_in, l0_shift_mu, l0_w0, l0_w2, l0_a0, l0_a2, l0_g2,
              l0_k_k, l0_k_a, l0_r_k, l0_ln_w, l0_ln_b, l0_moba_gain, l0_w_out, l0_norm_ffn,
              l0_ffn_gate, l0_ffn_up, l0_ffn_down,
              l1_mod_w, l1_mod_b, l1_norm_mix, l1_w_in, l1_shift_mu, l1_w0, l1_w2, l1_a0, l1_a2, l1_g2,
              l1_v0, l1_v2, l1_k_k, l1_k_a, l1_r_k, l1_ln_w, l1_ln_b, l1_moba_gain, l1_w_out, l1_norm_ffn,
              l1_router, l1_exp_gate, l1_exp_up, l1_exp_down,
              norm_out):
    layers = (
        dict(mod_w=l0_mod_w, mod_b=l0_mod_b, norm_mix=l0_norm_mix, w_in=l0_w_in, shift_mu=l0_shift_mu,
             w0=l0_w0, w2=l0_w2, a0=l0_a0, a2=l0_a2, g2=l0_g2, k_k=l0_k_k, k_a=l0_k_a, r_k=l0_r_k,
             ln_w=l0_ln_w, ln_b=l0_ln_b, moba_gain=l0_moba_gain, w_out=l0_w_out, norm_ffn=l0_norm_ffn,
             ffn_gate=l0_ffn_gate, ffn_up=l0_ffn_up, ffn_down=l0_ffn_down),
        dict(mod_w=l1_mod_w, mod_b=l1_mod_b, norm_mix=l1_norm_mix, w_in=l1_w_in, shift_mu=l1_shift_mu,
             w0=l1_w0, w2=l1_w2, a0=l1_a0, a2=l1_a2, g2=l1_g2, v0=l1_v0, v2=l1_v2, k_k=l1_k_k,
             k_a=l1_k_a, r_k=l1_r_k, ln_w=l1_ln_w, ln_b=l1_ln_b, moba_gain=l1_moba_gain, w_out=l1_w_out,
             norm_ffn=l1_norm_ffn, router=l1_router, exp_gate=l1_exp_gate, exp_up=l1_exp_up,
             exp_down=l1_exp_down),
    )
    c_act = jax.nn.silu(c)
    v_first = None
    for i in range(DEPTH):
        p = layers[i]
        mod = c_act @ p["mod_w"] + p["mod_b"]
        shift_m, scale_m, gate_m, shift_f, scale_f, gate_f = jnp.split(mod, 6, axis=-1)
        h = modulate(rms_norm(x, p["norm_mix"]), shift_m, scale_m)
        mix, v_first = mixer_sublayer(h, v_first, p)
        x = x + gate_m[:, None, :] * mix
        h = modulate(rms_norm(x, p["norm_ffn"]), shift_f, scale_f)
        if i % 2 == 0:
            f = swiglu(h, p["ffn_gate"], p["ffn_up"], p["ffn_down"])
        else:
            f = moe_swiglu(h, p["router"], p["exp_gate"], p["exp_up"], p["exp_down"])
        x = x + gate_f[:, None, :] * f
    return rms_norm(x, norm_out)
```

```python
import functools

import jax
import jax.numpy as jnp
from jax import lax
from jax.experimental import pallas as pl
from jax.experimental.pallas import tpu as pltpu

F32 = jnp.float32
BF16 = jnp.bfloat16

RMS_EPS = 1e-6
GN_EPS = 64e-5
RWKV_HEAD_DIM = 64
MOBA_HEAD_DIM = 128
MOBA_BLOCK = 256
MOBA_TOPK = 3
TOP_K = 2
LANES = 128
RWKV_CHUNK = 64
VMEM_LIMIT = 52 * 1024 * 1024

LORA_W_OFF, LORA_W_PAD = 0, 128
LORA_A_OFF, LORA_A_PAD = 128, 128
LORA_G_OFF, LORA_G_PAD = 256, 256
LORA_V_OFF, LORA_V_PAD = 512, 128
LORA_USED = 640
LORA_SECTION = 1024


def _cparams(sem):
    return pltpu.CompilerParams(dimension_semantics=sem, vmem_limit_bytes=VMEM_LIMIT)


def _bdot(a, b):
    return jnp.dot(a.astype(BF16), b.astype(BF16), preferred_element_type=F32)


def _bdot_nt(a, b):
    return lax.dot_general(a.astype(BF16), b.astype(BF16), (((1,), (1,)), ((), ())),
                           preferred_element_type=F32)


def _bdot_tn(a, b):
    return lax.dot_general(a.astype(BF16), b.astype(BF16), (((0,), (0,)), ((), ())),
                           preferred_element_type=F32)


def _split(x, terms):
    parts = []
    rem = x
    for _ in range(terms):
        hi = rem.astype(BF16)
        parts.append(hi)
        rem = rem - hi.astype(F32)
    return parts


def _dot_xe(x, e_bf16, terms=3):
    acc = None
    for part in _split(x, terms):
        d = jnp.dot(part, e_bf16, preferred_element_type=F32)
        acc = d if acc is None else acc + d
    return acc


def _dot_ex(e_bf16, x, terms=3):
    acc = None
    for part in _split(x, terms):
        d = jnp.dot(e_bf16, part, preferred_element_type=F32)
        acc = d if acc is None else acc + d
    return acc


def _dot3(a, b, nt=False):
    a_hi, a_lo = _split(a, 2)
    b_hi, b_lo = _split(b, 2)
    if nt:
        f = lambda u, v: lax.dot_general(u, v, (((1,), (1,)), ((), ())), preferred_element_type=F32)
    else:
        f = lambda u, v: jnp.dot(u, v, preferred_element_type=F32)
    return f(a_hi, b_hi) + (f(a_lo, b_hi) + f(a_hi, b_lo))


def _norm_mod(x, gain, shift, scale):
    y = x * lax.rsqrt(jnp.mean(x * x, axis=-1, keepdims=True) + RMS_EPS)
    return (y * gain) * (1.0 + scale) + shift


def _mod_kernel(c_ref, w_ref, b_ref, o_ref):
    c = c_ref[...]
    o_ref[...] = _dot3(c * jax.nn.sigmoid(c), w_ref[...]) + b_ref[...]


def _adaln(c_pad, mod_w, mod_b, tn=512):
    m, d = c_pad.shape
    n = mod_w.shape[1]
    return pl.pallas_call(
        _mod_kernel,
        grid=(n // tn,),
        in_specs=[pl.BlockSpec((m, d), lambda j: (0, 0)),
                  pl.BlockSpec((d, tn), lambda j: (0, j)),
                  pl.BlockSpec((1, tn), lambda j: (0, j))],
        out_specs=pl.BlockSpec((m, tn), lambda j: (0, j)),
        out_shape=jax.ShapeDtypeStruct((m, n), F32),
        compiler_params=_cparams(("arbitrary",)),
        name="adaln",
    )(c_pad, mod_w, mod_b.reshape(1, n))


def _mod_specs(tm, seq, d, shift_idx, scale_idx):
    sh = pl.BlockSpec((None, 1, d), lambda i, j: ((i * tm) // seq * 6 + shift_idx, 0, 0))
    sc = pl.BlockSpec((None, 1, d), lambda i, j: ((i * tm) // seq * 6 + scale_idx, 0, 0))
    return sh, sc


def _in_proj_kernel(x_ref, g_ref, sh_ref, sc_ref, w_ref, o_ref, h_scr):
    @pl.when(pl.program_id(1) == 0)
    def _():
        h_scr[...] = _norm_mod(x_ref[...], g_ref[...], sh_ref[...], sc_ref[...]).astype(BF16)

    o_ref[...] = jnp.dot(h_scr[...], w_ref[...].astype(BF16), preferred_element_type=F32)


def _in_proj(x2, gain, mod3, w, seq, shift_idx, scale_idx, tm=512, tn=512):
    m, d = x2.shape
    n = w.shape[1]
    sh, sc = _mod_specs(tm, seq, d, shift_idx, scale_idx)
    return pl.pallas_call(
        _in_proj_kernel,
        grid=(m // tm, n // tn),
        in_specs=[pl.BlockSpec((tm, d), lambda i, j: (i, 0)),
                  pl.BlockSpec((1, d), lambda i, j: (0, 0)),
                  sh, sc,
                  pl.BlockSpec((d, tn), lambda i, j: (0, j))],
        out_specs=pl.BlockSpec((tm, tn), lambda i, j: (i, j)),
        out_shape=jax.ShapeDtypeStruct((m, n), F32),
        scratch_shapes=[pltpu.VMEM((tm, d), BF16)],
        compiler_params=_cparams(("arbitrary", "arbitrary")),
        name="in_proj",
    )(x2, gain.reshape(1, d), mod3, mod3, w)


def _ffn_up_kernel(x_ref, g_ref, sh_ref, sc_ref, wg_ref, wu_ref, o_ref, h_scr):
    @pl.when(pl.program_id(1) == 0)
    def _():
        h_scr[...] = _norm_mod(x_ref[...], g_ref[...], sh_ref[...], sc_ref[...]).astype(BF16)

    h = h_scr[...]
    gate = jnp.dot(h, wg_ref[...].astype(BF16), preferred_element_type=F32)
    up = jnp.dot(h, wu_ref[...].astype(BF16), preferred_element_type=F32)
    o_ref[...] = (jax.nn.silu(gate) * up).astype(BF16)


def _ffn_up(x2, gain, mod3, wg, wu, seq, shift_idx, scale_idx, tm=512, tn=512):
    m, d = x2.shape
    n = wg.shape[1]
    sh, sc = _mod_specs(tm, seq, d, shift_idx, scale_idx)
    return pl.pallas_call(
        _ffn_up_kernel,
        grid=(m // tm, n // tn),
        in_specs=[pl.BlockSpec((tm, d), lambda i, j: (i, 0)),
                  pl.BlockSpec((1, d), lambda i, j: (0, 0)),
                  sh, sc,
                  pl.BlockSpec((d, tn), lambda i, j: (0, j)),
                  pl.BlockSpec((d, tn), lambda i, j: (0, j))],
        out_specs=pl.BlockSpec((tm, tn), lambda i, j: (i, j)),
        out_shape=jax.ShapeDtypeStruct((m, n), BF16),
        scratch_shapes=[pltpu.VMEM((tm, d), BF16)],
        compiler_params=_cparams(("arbitrary", "arbitrary")),
        name="ffn_up",
    )(x2, gain.reshape(1, d), mod3, mod3, wg, wu)


def _mm_res_kernel(*refs, n_lhs, k_chunk):
    lhs = refs[:n_lhs]
    ws = refs[n_lhs:2 * n_lhs]
    res_ref, gate_ref, o_ref = refs[2 * n_lhs:]
    acc = None
    for l_ref, w_ref in zip(lhs, ws):
        k = l_ref.shape[1]
        for k0 in range(0, k, k_chunk):
            d = jnp.dot(l_ref[:, k0:k0 + k_chunk].astype(BF16),
                        w_ref[k0:k0 + k_chunk, :].astype(BF16), preferred_element_type=F32)
            acc = d if acc is None else acc + d
    o_ref[...] = res_ref[...] + gate_ref[...] * acc


def _mm_res(lhs_list, w, res, mod3, seq, gate_idx, k_chunk, tm=512, tn=512):
    m, n = res.shape
    k = lhs_list[0].shape[1]
    n_lhs = len(lhs_list)
    in_specs = [pl.BlockSpec((tm, k), lambda j, i: (i, 0)) for _ in lhs_list]
    in_specs += [pl.BlockSpec((k, tn), functools.partial(lambda j, i, li: (li, j), li=li))
                 for li in range(n_lhs)]
    in_specs += [pl.BlockSpec((tm, tn), lambda j, i: (i, j)),
                 pl.BlockSpec((None, 1, tn), lambda j, i: ((i * tm) // seq * 6 + gate_idx, 0, j))]
    return pl.pallas_call(
        functools.partial(_mm_res_kernel, n_lhs=n_lhs, k_chunk=k_chunk),
        grid=(n // tn, m // tm),
        in_specs=in_specs,
        out_specs=pl.BlockSpec((tm, tn), lambda j, i: (i, j)),
        out_shape=jax.ShapeDtypeStruct((m, n), F32),
        compiler_params=_cparams(("arbitrary", "arbitrary")),
        name="mm_res",
    )(*lhs_list, *([w] * n_lhs), res, mod3)


def _rwkv_kernel(*refs, chunk, width, vres):
    it = iter(refs)
    p_ref, mu_ref, w0_ref, w2_ref, a0_ref, a2_ref, g2_ref = (next(it) for _ in range(7))
    if vres:
        v0_ref, v2_ref, vf_ref = next(it), next(it), next(it)
    kk_ref, ka_ref, rk_ref, lnw_ref, lnb_ref = (next(it) for _ in range(5))
    o_ref = next(it)
    vf_out_ref = None if vres else next(it)
    carry_scr, state_scr = next(it), next(it)

    L, C = chunk, width
    P2 = 2 * L
    ci = pl.program_id(1)

    @pl.when(ci == 0)
    def _():
        carry_scr[...] = jnp.zeros_like(carry_scr)
        state_scr[...] = jnp.zeros_like(state_scr)

    def shifted(lo, hi):
        p = p_ref[:, lo:hi]
        rolled = pltpu.roll(p, 1, 0)
        row = lax.broadcasted_iota(jnp.int32, p.shape, 0)
        prev = jnp.where(row == 0, carry_scr[7:8, lo:hi], rolled)
        return p + (prev - p) * mu_ref[:, lo:hi]

    lo = 3 * C
    zw = shifted(lo + LORA_W_OFF, lo + LORA_W_OFF + LORA_W_PAD)
    za = shifted(lo + LORA_A_OFF, lo + LORA_A_OFF + LORA_A_PAD)
    zg = shifted(lo + LORA_G_OFF, lo + LORA_G_OFF + LORA_G_PAD)
    wl = w0_ref[...] + _bdot(jnp.tanh(zw), w2_ref[...])
    logdecay = -jnp.exp(-jax.nn.softplus(-wl) - 0.5)
    a_all = jax.nn.sigmoid(a0_ref[...] + _bdot(za, a2_ref[...]))
    g_all = _bdot(jax.nn.sigmoid(zg), g2_ref[...])
    if vres:
        zv = shifted(lo + LORA_V_OFF, lo + LORA_V_OFF + LORA_V_PAD)
        vmix = jax.nn.sigmoid(v0_ref[...] + _bdot(zv, v2_ref[...]))

    tr = lax.broadcasted_iota(jnp.int32, (L, L), 0)
    tc = lax.broadcasted_iota(jnp.int32, (L, L), 1)
    tri = (tc <= tr).astype(BF16)
    cum = _dot_ex(tri, logdecay)
    p_inc = jnp.exp(cum)
    p_exc = jnp.exp(cum - logdecay)
    p_inv = jnp.exp(-cum)

    rr = lax.broadcasted_iota(jnp.int32, (P2, P2), 0)
    cc = lax.broadcasted_iota(jnp.int32, (P2, P2), 1)
    strict = cc < rr
    incl = cc <= rr
    eye = (cc == rr).astype(F32)
    lane = lax.broadcasted_iota(jnp.int32, (L, LANES), 1)
    head0 = lane < RWKV_HEAD_DIM
    seg = ((rr // RWKV_HEAD_DIM) == (cc // RWKV_HEAD_DIM)).astype(BF16)

    def expand(a):
        return jnp.concatenate([jnp.where(head0, a, 0.0), jnp.where(head0, 0.0, a)], axis=0)

    for hp in range(C // LANES):
        sl = slice(hp * LANES, (hp + 1) * LANES)
        r = shifted(hp * LANES, (hp + 1) * LANES)
        k = shifted(C + hp * LANES, C + (hp + 1) * LANES)
        v = shifted(2 * C + hp * LANES, 2 * C + (hp + 1) * LANES)
        if vres:
            v = v + (vf_ref[:, sl] - v) * vmix[:, sl]
        else:
            vf_out_ref[:, sl] = v
        a = a_all[:, sl]
        kk = k * kk_ref[:, sl]
        kk = kk * lax.rsqrt(jnp.maximum(_dot_xe(kk * kk, seg), 1e-24))
        k = k * (1.0 + (a - 1.0) * ka_ref[:, sl])

        ax = expand(-kk * p_exc[:, sl]).astype(BF16)
        rx = expand(r * p_inc[:, sl]).astype(BF16)
        bx = expand(kk * a * p_inv[:, sl]).astype(BF16)
        kx = expand(k * p_inv[:, sl]).astype(BF16)
        vx = expand(v).astype(BF16)

        a_ab = jnp.where(strict, _bdot_nt(ax, bx), 0.0)
        a_ak = jnp.where(strict, _bdot_nt(ax, kx), 0.0)
        a_rb = jnp.where(incl, _bdot_nt(rx, bx), 0.0)
        a_rk = jnp.where(incl, _bdot_nt(rx, kx), 0.0)

        npow = a_ab
        tinv = eye + npow
        span = 2
        while span < L:
            npow = _bdot(npow, npow)
            tinv = tinv + _bdot(tinv, npow)
            span *= 2

        w1 = _bdot(tinv, ax)
        w2 = _bdot(tinv, _bdot(a_ak, vx))
        s0 = state_scr[hp]
        u = _bdot_nt(w1, s0) + w2
        y2 = _bdot_nt(rx, s0) + _bdot(a_rb, u) + _bdot(a_rk, vx)
        s_new = s0 + _bdot_tn(u, bx) + _bdot_tn(vx, kx)
        state_scr[hp] = s_new * p_inc[L - 1:L, sl]
        y = y2[:L] + y2[L:]

        mean = _dot_xe(y, seg) * (1.0 / RWKV_HEAD_DIM)
        dev = y - mean
        var = _dot_xe(dev * dev, seg) * (1.0 / RWKV_HEAD_DIM)
        yn = dev * lax.rsqrt(var + GN_EPS) * lnw_ref[:, sl] + lnb_ref[:, sl]
        bonus = _dot_xe(r * k * rk_ref[:, sl], seg) * v
        o_ref[:, sl] = (yn + bonus) * g_all[:, sl]

    carry_scr[...] = p_ref[L - 8:L, :]


def _rwkv(proj, batch, seq, lp, v_first):
    vres = v_first is not None
    C = lp["w0"].shape[0]
    L = RWKV_CHUNK
    W = 3 * C + LORA_USED
    row = lambda a: a.reshape(1, -1)
    full = lambda a: pl.BlockSpec(a.shape, lambda b, c: (0, 0))
    tok = pl.BlockSpec((L, C), lambda b, c: (b * (seq // L) + c, 0))
    args = [proj, lp["mu_pad"], row(lp["w0"]), lp["w2_pad"], row(lp["a0"]), lp["a2_pad"], lp["g2_pad"]]
    in_specs = [pl.BlockSpec((L, W), lambda b, c: (b * (seq // L) + c, 0))]
    in_specs += [full(a) for a in args[1:]]
    if vres:
        extra = [row(lp["v0"]), lp["v2_pad"]]
        args += extra + [v_first]
        in_specs += [full(a) for a in extra] + [tok]
    tail = [row(lp["k_k"]), row(lp["k_a"]), row(lp["r_k"]), row(lp["ln_w"]), row(lp["ln_b"])]
    args += tail
    in_specs += [full(a) for a in tail]
    out_sd = jax.ShapeDtypeStruct((batch * seq, C), F32)
    out = pl.pallas_call(
        functools.partial(_rwkv_kernel, chunk=L, width=C, vres=vres),
        grid=(batch, seq // L),
        in_specs=in_specs,
        out_specs=tok if vres else (tok, tok),
        out_shape=out_sd if vres else (out_sd, out_sd),
        scratch_shapes=[pltpu.VMEM((8, W), F32),
                        pltpu.VMEM((C // LANES, LANES, LANES), F32)],
        compiler_params=_cparams(("arbitrary", "arbitrary")),
        name="rwkv7",
    )(*args)
    return (out, v_first) if vres else out


def _moba_kernel(q_ref, k_ref, v_ref, slope_ref, gain_ref, o_ref, kmean_scr, *, n_blocks):
    BS = MOBA_BLOCK
    qb = pl.program_id(2)

    @pl.when(qb == 0)
    def _():
        kmean_scr[...] = jnp.zeros_like(kmean_scr)
        for j in range(n_blocks):
            kmean_scr[j:j + 1, :] = jnp.mean(k_ref[j * BS:(j + 1) * BS, :], axis=0, keepdims=True)

    q = q_ref[...]
    col = lax.broadcasted_iota(jnp.int32, (BS, LANES), 1)
    gate = _dot3(q, kmean_scr[...], nt=True)
    gate = jnp.where(col < qb, gate, -jnp.inf)
    beaten = jnp.zeros((BS, LANES), jnp.int32)
    for j in range(n_blocks):
        gj = gate[:, j:j + 1]
        ahead = (gj > gate) | ((gj == gate) & (col > j))
        beaten = beaten + ahead.astype(jnp.int32)
    sel = ((beaten < MOBA_TOPK) & (col < qb)).astype(BF16)

    slope = slope_ref[...]
    scale = MOBA_HEAD_DIM ** -0.5
    qh = q.astype(BF16)

    for own in range(n_blocks):
        @pl.when(qb == own)
        def _(own=own):
            nk = (own + 1) * BS
            row = lax.broadcasted_iota(jnp.int32, (BS, nk), 0)
            kcol = lax.broadcasted_iota(jnp.int32, (BS, nk), 1)
            s = _bdot_nt(qh, k_ref[0:nk, :]) * scale
            dist = (own * BS + row - kcol).astype(F32)
            s = s - slope[:, 0:1] * dist
            allowed = (kcol >= own * BS) & (kcol - own * BS <= row)
            if own > 0:
                blk = lax.broadcasted_iota(jnp.int32, (LANES, nk), 0)
                bcol = lax.broadcasted_iota(jnp.int32, (LANES, nk), 1)
                ind = (bcol // BS == blk).astype(BF16)
                picked = jnp.dot(sel, ind, preferred_element_type=F32) > 0.5
                allowed = allowed | ((kcol < own * BS) & picked)
            s = jnp.where(allowed, s, -jnp.inf)
            m = jnp.max(s, axis=-1, keepdims=True)
            e = jnp.exp(s - m)
            denom = jnp.sum(e, axis=-1, keepdims=True)
            out = _bdot(e / denom, v_ref[0:nk, :])
            out = out * lax.rsqrt(jnp.mean(out * out, axis=-1, keepdims=True) + RMS_EPS)
            o_ref[...] = out * gain_ref[...]


def _moba(proj, batch, seq, q_off, gain):
    width = gain.shape[0]
    heads = width // MOBA_HEAD_DIM
    BS = MOBA_BLOCK
    nb = seq // BS
    qo = q_off // LANES
    slopes = jnp.exp2(-8.0 * jnp.arange(1, heads + 1, dtype=F32) / heads)
    slopes = jnp.broadcast_to(slopes[:, None, None], (heads, 1, LANES))
    return pl.pallas_call(
        functools.partial(_moba_kernel, n_blocks=nb),
        grid=(batch, heads, nb),
        in_specs=[pl.BlockSpec((BS, LANES), lambda b, h, i: (b * nb + i, qo + h)),
                  pl.BlockSpec((seq, LANES), lambda b, h, i: (b, qo + heads + h)),
                  pl.BlockSpec((seq, LANES), lambda b, h, i: (b, qo + 2 * heads + h)),
                  pl.BlockSpec((None, 1, LANES), lambda b, h, i: (h, 0, 0)),
                  pl.BlockSpec((1, LANES), lambda b, h, i: (0, h))],
        out_specs=pl.BlockSpec((BS, LANES), lambda b, h, i: (b * nb + i, h)),
        out_shape=jax.ShapeDtypeStruct((batch * seq, width), F32),
        scratch_shapes=[pltpu.VMEM((LANES, LANES), F32)],
        compiler_params=_cparams(("arbitrary", "arbitrary", "arbitrary")),
        name="moba",
    )(proj, proj, proj, slopes, gain.reshape(1, width))


def _router_kernel(x_ref, g_ref, sh_ref, sc_ref, wr_ref, h_ref, route_ref, *, n_experts):
    h = _norm_mod(x_ref[...], g_ref[...], sh_ref[...], sc_ref[...])
    h_ref[...] = h.astype(BF16)
    logits = _dot3(h, wr_ref[...])
    col = lax.broadcasted_iota(jnp.int32, logits.shape, 1)
    neg = -jnp.inf
    logits = jnp.where(col < n_experts, logits, neg)
    m1 = jnp.max(logits, axis=-1, keepdims=True)
    i1 = jnp.min(jnp.where(logits == m1, col, LANES), axis=-1, keepdims=True)
    rest = jnp.where(col == i1, neg, logits)
    m2 = jnp.max(rest, axis=-1, keepdims=True)
    i2 = jnp.min(jnp.where(rest == m2, col, LANES), axis=-1, keepdims=True)
    e2 = jnp.exp(m2 - m1)
    wa = 1.0 / (1.0 + e2)
    wb = e2 / (1.0 + e2)
    route = jnp.where(col == 0, i1.astype(F32), 0.0)
    route = jnp.where(col == 1, i2.astype(F32), route)
    route = jnp.where(col == 2, wa, route)
    route = jnp.where(col == 3, wb, route)
    route_ref[...] = route


def _router(x2, gain, mod3, router_pad, seq, shift_idx, scale_idx, n_experts, tm=512):
    m, d = x2.shape
    sh = pl.BlockSpec((None, 1, d), lambda i: ((i * tm) // seq * 6 + shift_idx, 0, 0))
    sc = pl.BlockSpec((None, 1, d), lambda i: ((i * tm) // seq * 6 + scale_idx, 0, 0))
    return pl.pallas_call(
        functools.partial(_router_kernel, n_experts=n_experts),
        grid=(m // tm,),
        in_specs=[pl.BlockSpec((tm, d), lambda i: (i, 0)),
                  pl.BlockSpec((1, d), lambda i: (0, 0)),
                  sh, sc,
                  pl.BlockSpec((d, LANES), lambda i: (0, 0))],
        out_specs=(pl.BlockSpec((tm, d), lambda i: (i, 0)),
                   pl.BlockSpec((tm, LANES), lambda i: (i, 0))),
        out_shape=(jax.ShapeDtypeStruct((m, d), BF16),
                   jax.ShapeDtypeStruct((m, LANES), F32)),
        compiler_params=_cparams(("arbitrary",)),
        name="router",
    )(x2, gain.reshape(1, d), mod3, mod3, router_pad)


def _moe_up_kernel(te_ref, tv_ref, x_ref, wg_ref, wu_ref, o_ref):
    i = pl.program_id(1)

    @pl.when(tv_ref[i] > 0)
    def _():
        x = x_ref[...]
        gate = jnp.dot(x, wg_ref[...].astype(BF16), preferred_element_type=F32)
        up = jnp.dot(x, wu_ref[...].astype(BF16), preferred_element_type=F32)
        o_ref[...] = (jax.nn.silu(gate) * up).astype(BF16)

    @pl.when(tv_ref[i] == 0)
    def _():
        o_ref[...] = jnp.zeros_like(o_ref)


def _moe_up(tile_expert, tile_valid, xs, wg, wu, tm, tf=512):
    p, d = xs.shape
    f = wg.shape[2]
    grid_spec = pltpu.PrefetchScalarGridSpec(
        num_scalar_prefetch=2,
        grid=(f // tf, p // tm),
        in_specs=[pl.BlockSpec((tm, d), lambda j, i, te, tv: (i, 0)),
                  pl.BlockSpec((None, d, tf), lambda j, i, te, tv: (te[i], 0, j)),
                  pl.BlockSpec((None, d, tf), lambda j, i, te, tv: (te[i], 0, j))],
        out_specs=pl.BlockSpec((tm, tf), lambda j, i, te, tv: (i, j)),
    )
    return pl.pallas_call(
        _moe_up_kernel,
        grid_spec=grid_spec,
        out_shape=jax.ShapeDtypeStruct((p, f), BF16),
        compiler_params=_cparams(("arbitrary", "arbitrary")),
        name="moe_up",
    )(tile_expert, tile_valid, xs, wg, wu)


def _moe_down_kernel(te_ref, tv_ref, a_ref, wd_ref, o_ref, *, k_chunk):
    i = pl.program_id(1)

    @pl.when(tv_ref[i] > 0)
    def _():
        acc = None
        for k0 in range(0, a_ref.shape[1], k_chunk):
            d = jnp.dot(a_ref[:, k0:k0 + k_chunk], wd_ref[k0:k0 + k_chunk, :].astype(BF16),
                        preferred_element_type=F32)
            acc = d if acc is None else acc + d
        o_ref[...] = acc

    @pl.when(tv_ref[i] == 0)
    def _():
        o_ref[...] = jnp.zeros_like(o_ref)


def _moe_down(tile_expert, tile_valid, act, wd, tm, k_chunk, tn=512):
    p, f = act.shape
    n = wd.shape[2]
    grid_spec = pltpu.PrefetchScalarGridSpec(
        num_scalar_prefetch=2,
        grid=(n // tn, p // tm),
        in_specs=[pl.BlockSpec((tm, f), lambda j, i, te, tv: (i, 0)),
                  pl.BlockSpec((None, f, tn), lambda j, i, te, tv: (te[i], 0, j))],
        out_specs=pl.BlockSpec((tm, tn), lambda j, i, te, tv: (i, j)),
    )
    return pl.pallas_call(
        functools.partial(_moe_down_kernel, k_chunk=k_chunk),
        grid_spec=grid_spec,
        out_shape=jax.ShapeDtypeStruct((p, n), F32),
        compiler_params=_cparams(("arbitrary", "arbitrary")),
        name="moe_down",
    )(tile_expert, tile_valid, act, wd)


def _moe_combine_kernel(x_ref, ya_ref, yb_ref, route_ref, gate_ref, o_ref):
    route = route_ref[...]
    f = route[:, 2:3] * ya_ref[...] + route[:, 3:4] * yb_ref[...]
    o_ref[...] = x_ref[...] + gate_ref[...] * f


def _moe_combine(x2, ya, yb, route, mod3, seq, gate_idx, tm=512):
    m, d = x2.shape
    return pl.pallas_call(
        _moe_combine_kernel,
        grid=(m // tm,),
        in_specs=[pl.BlockSpec((tm, d), lambda i: (i, 0)),
                  pl.BlockSpec((tm, d), lambda i: (i, 0)),
                  pl.BlockSpec((tm, d), lambda i: (i, 0)),
                  pl.BlockSpec((tm, LANES), lambda i: (i, 0)),
                  pl.BlockSpec((None, 1, d), lambda i: ((i * tm) // seq * 6 + gate_idx, 0, 0))],
        out_specs=pl.BlockSpec((tm, d), lambda i: (i, 0)),
        out_shape=jax.ShapeDtypeStruct((m, d), F32),
        compiler_params=_cparams(("arbitrary",)),
        name="moe_combine",
    )(x2, ya, yb, route, mod3)


def _moe(x2, gain, mod3, router, wg, wu, wd, seq, tm=512):
    m, d = x2.shape
    n_experts = router.shape[1]
    router_pad = jnp.pad(router, ((0, 0), (0, LANES - n_experts)))
    h, route = _router(x2, gain, mod3, router_pad, seq, 3, 4, n_experts)

    slots = m * TOP_K
    n_tiles = slots // tm + n_experts
    flat_e = route[:, 0:TOP_K].astype(jnp.int32).reshape(slots)
    onehot = (flat_e[:, None] == jnp.arange(n_experts)[None, :]).astype(jnp.int32)
    rank = jnp.sum((jnp.cumsum(onehot, axis=0) - onehot) * onehot, axis=1)
    counts = jnp.sum(onehot, axis=0)
    padded = (counts + tm - 1) // tm * tm
    ends = jnp.cumsum(padded)
    dest = (ends - padded)[flat_e] + rank
    sorted_tok = jnp.zeros((n_tiles * tm,), jnp.int32).at[dest].set(jnp.arange(slots, dtype=jnp.int32) // TOP_K)
    tile_start = jnp.arange(n_tiles, dtype=jnp.int32) * tm
    tile_expert = jnp.minimum(jnp.searchsorted(ends, tile_start, side="right"), n_experts - 1).astype(jnp.int32)
    tile_valid = (tile_start < ends[-1]).astype(jnp.int32)

    xs = jnp.take(h, sorted_tok, axis=0)
    act = _moe_up(tile_expert, tile_valid, xs, wg, wu, tm)
    ys = _moe_down(tile_expert, tile_valid, act, wd, tm, k_chunk=wd.shape[1] // 4)
    dest2 = dest.reshape(m, TOP_K)
    ya = jnp.take(ys, dest2[:, 0], axis=0)
    yb = jnp.take(ys, dest2[:, 1], axis=0)
    return _moe_combine(x2, ya, yb, route, mod3, seq, 5)


def _final_norm_kernel(x_ref, g_ref, o_ref):
    x = x_ref[...]
    o_ref[...] = x * lax.rsqrt(jnp.mean(x * x, axis=-1, keepdims=True) + RMS_EPS) * g_ref[...]


def _final_norm(x2, gain, tm=512):
    m, d = x2.shape
    return pl.pallas_call(
        _final_norm_kernel,
        grid=(m // tm,),
        in_specs=[pl.BlockSpec((tm, d), lambda i: (i, 0)),
                  pl.BlockSpec((1, d), lambda i: (0, 0))],
        out_specs=pl.BlockSpec((tm, d), lambda i: (i, 0)),
        out_shape=jax.ShapeDtypeStruct((m, d), F32),
        compiler_params=_cparams(("arbitrary",)),
        name="final_norm",
    )(x2, gain.reshape(1, d))


def _pad_cols(a, n):
    return jnp.pad(a, ((0, 0), (0, n - a.shape[1])))


def _pad_rows(a, n):
    return jnp.pad(a, ((0, n - a.shape[0]), (0, 0)))


def _prep_mixer(lp, C):
    vres = "v0" in lp
    w_in, mu = lp["w_in"], lp["shift_mu"]
    sizes = [3 * C, 64, 64, 160] + ([32] if vres else [])
    pads = [3 * C, LORA_W_PAD, LORA_A_PAD, LORA_G_PAD] + ([LORA_V_PAD] if vres else [])
    cols, mus, off = [], [], 0
    for s, pw in zip(sizes, pads):
        cols.append(_pad_cols(w_in[:, off:off + s], pw))
        mus.append(jnp.pad(mu[off:off + s], (0, pw - s)))
        off += s
    n_r = off
    w_r = _pad_cols(jnp.concatenate(cols, axis=1), 3 * C + LORA_SECTION)
    out = dict(lp)
    out["w_in_cat"] = jnp.concatenate([w_r, w_in[:, n_r:]], axis=1).astype(BF16)
    out["mu_pad"] = jnp.pad(jnp.concatenate(mus), (0, 3 * C + LORA_USED - sum(pads))).reshape(1, -1)
    out["w2_pad"] = _pad_rows(lp["w2"], LORA_W_PAD)
    out["a2_pad"] = _pad_rows(lp["a2"], LORA_A_PAD)
    out["g2_pad"] = _pad_rows(lp["g2"], LORA_G_PAD)
    if vres:
        out["v2_pad"] = _pad_rows(lp["v2"], LORA_V_PAD)
    return out


def _layer(x2, mod3, lp, batch, seq, v_first):
    C = lp["w0"].shape[0]
    lp = _prep_mixer(lp, C)
    proj = _in_proj(x2, lp["norm_mix"], mod3, lp["w_in_cat"], seq, 0, 1)
    if v_first is None:
        y_r, v_first = _rwkv(proj, batch, seq, lp, None)
    else:
        y_r, v_first = _rwkv(proj, batch, seq, lp, v_first)
    y_m = _moba(proj, batch, seq, 3 * C + LORA_SECTION, lp["moba_gain"])
    x2 = _mm_res([y_r, y_m], lp["w_out"], x2, mod3, seq, 2, k_chunk=y_r.shape[1])
    if "ffn_gate" in lp:
        act = _ffn_up(x2, lp["norm_ffn"], mod3, lp["ffn_gate"], lp["ffn_up"], seq, 3, 4)
        x2 = _mm_res([act], lp["ffn_down"], x2, mod3, seq, 5, k_chunk=act.shape[1] // 4)
    else:
        x2 = _moe(x2, lp["norm_ffn"], mod3, lp["router"], lp["exp_gate"], lp["exp_up"],
                  lp["exp_down"], seq)
    return x2, v_first


def _forward(x, c, layers, norm_out):
    batch, seq, d = x.shape
    x2 = x.reshape(batch * seq, d)
    c_pad = jnp.pad(c, ((0, 8 - batch), (0, 0)))
    v_first = None
    for lp in layers:
        mod = _adaln(c_pad, lp["mod_w"], lp["mod_b"])[:batch]
        mod3 = mod.reshape(batch * 6, 1, d)
        x2, v_first = _layer(x2, mod3, lp, batch, seq, v_first)
    return _final_norm(x2, norm_out).reshape(batch, seq, d)


def kernel(x, c, l0_mod_w, l0_mod_b, l0_norm_mix, l0_w_in, l0_shift_mu, l0_w0, l0_w2, l0_a0, l0_a2, l0_g2, l0_k_k, l0_k_a, l0_r_k, l0_ln_w, l0_ln_b, l0_moba_gain, l0_w_out, l0_norm_ffn, l0_ffn_gate, l0_ffn_up, l0_ffn_down, l1_mod_w, l1_mod_b, l1_norm_mix, l1_w_in, l1_shift_mu, l1_w0, l1_w2, l1_a0, l1_a2, l1_g2, l1_v0, l1_v2, l1_k_k, l1_k_a, l1_r_k, l1_ln_w, l1_ln_b, l1_moba_gain, l1_w_out, l1_norm_ffn, l1_router, l1_exp_gate, l1_exp_up, l1_exp_down, norm_out):
    layers = (
        dict(mod_w=l0_mod_w, mod_b=l0_mod_b, norm_mix=l0_norm_mix, w_in=l0_w_in, shift_mu=l0_shift_mu,
             w0=l0_w0, w2=l0_w2, a0=l0_a0, a2=l0_a2, g2=l0_g2, k_k=l0_k_k, k_a=l0_k_a, r_k=l0_r_k,
             ln_w=l0_ln_w, ln_b=l0_ln_b, moba_gain=l0_moba_gain, w_out=l0_w_out, norm_ffn=l0_norm_ffn,
             ffn_gate=l0_ffn_gate, ffn_up=l0_ffn_up, ffn_down=l0_ffn_down),
        dict(mod_w=l1_mod_w, mod_b=l1_mod_b, norm_mix=l1_norm_mix, w_in=l1_w_in, shift_mu=l1_shift_mu,
             w0=l1_w0, w2=l1_w2, a0=l1_a0, a2=l1_a2, g2=l1_g2, v0=l1_v0, v2=l1_v2, k_k=l1_k_k,
             k_a=l1_k_a, r_k=l1_r_k, ln_w=l1_ln_w, ln_b=l1_ln_b, moba_gain=l1_moba_gain, w_out=l1_w_out,
             norm_ffn=l1_norm_ffn, router=l1_router, exp_gate=l1_exp_gate, exp_up=l1_exp_up,
             exp_down=l1_exp_down),
    )
    return _forward(x, c, layers, norm_out)
```

```python
import functools

import jax
import jax.numpy as jnp
from jax import lax
from jax.experimental import pallas as pl
from jax.experimental.pallas import tpu as pltpu

F32 = jnp.float32
BF16 = jnp.bfloat16

RMS_EPS = 1e-6
GN_EPS = 64e-5
RWKV_HEAD_DIM = 64
MOBA_HEAD_DIM = 128
MOBA_BLOCK = 256
MOBA_TOPK = 3
TOP_K = 2
LANES = 128
SUBLANES = 8
RWKV_CHUNK = 64
VMEM_LIMIT = 52 * 1024 * 1024
CAST_ROWS = 256

LORA_WINDOW = 384
RWKV_PROJ_COLS = 3584


def _cparams(sem):
    return pltpu.CompilerParams(dimension_semantics=sem, vmem_limit_bytes=VMEM_LIMIT)


def _bdot(a, b):
    return jnp.dot(a.astype(BF16), b.astype(BF16), preferred_element_type=F32)


def _bdot_nt(a, b):
    return lax.dot_general(a.astype(BF16), b.astype(BF16), (((1,), (1,)), ((), ())),
                           preferred_element_type=F32)


def _bdot_tn(a, b):
    return lax.dot_general(a.astype(BF16), b.astype(BF16), (((0,), (0,)), ((), ())),
                           preferred_element_type=F32)


def _split(x, terms):
    parts = []
    rem = x
    for _ in range(terms):
        hi = rem.astype(BF16)
        parts.append(hi)
        rem = rem - hi.astype(F32)
    return parts


def _dot_xe(x, e_bf16, terms):
    acc = None
    for part in _split(x, terms):
        d = jnp.dot(part, e_bf16, preferred_element_type=F32)
        acc = d if acc is None else acc + d
    return acc


def _dot_ex(e_bf16, x, terms):
    acc = None
    for part in _split(x, terms):
        d = jnp.dot(e_bf16, part, preferred_element_type=F32)
        acc = d if acc is None else acc + d
    return acc


def _dot3(a, b, nt=False):
    a_hi, a_lo = _split(a, 2)
    b_hi, b_lo = _split(b, 2)
    if nt:
        f = lambda u, v: lax.dot_general(u, v, (((1,), (1,)), ((), ())), preferred_element_type=F32)
    else:
        f = lambda u, v: jnp.dot(u, v, preferred_element_type=F32)
    return f(a_hi, b_hi) + (f(a_lo, b_hi) + f(a_hi, b_lo))


def _norm_mod(x, gain, shift, scale):
    y = x * lax.rsqrt(jnp.mean(x * x, axis=-1, keepdims=True) + RMS_EPS)
    return (y * gain) * (1.0 + scale) + shift


def _cast_block(w_ref, s_ref):
    if w_ref.dtype == BF16:
        return
    rows = w_ref.shape[0]
    step = CAST_ROWS if rows % CAST_ROWS == 0 else rows

    def body(r, carry):
        sl = pl.ds(pl.multiple_of(r * step, step), step)
        s_ref[sl, :] = w_ref[sl, :].astype(BF16)
        return carry

    lax.fori_loop(0, rows // step, body, 0)


def _mod_kernel(c_ref, w_ref, b_ref, o_ref):
    c = c_ref[...]
    o_ref[...] = _dot3(c * jax.nn.sigmoid(c), w_ref[...]) + b_ref[...]


def _adaln(c_pad, mod_w, mod_b, tn=512):
    m, d = c_pad.shape
    n = mod_w.shape[1]
    return pl.pallas_call(
        _mod_kernel,
        grid=(n // tn,),
        in_specs=[pl.BlockSpec((m, d), lambda j: (0, 0)),
                  pl.BlockSpec((d, tn), lambda j: (0, j)),
                  pl.BlockSpec((1, tn), lambda j: (0, j))],
        out_specs=pl.BlockSpec((m, tn), lambda j: (0, j)),
        out_shape=jax.ShapeDtypeStruct((m, n), F32),
        compiler_params=_cparams(("arbitrary",)),
        name="adaln",
    )(c_pad, mod_w, mod_b.reshape(1, n))


def _normmod_kernel(x_ref, g_ref, sh_ref, sc_ref, o_ref):
    o_ref[...] = _norm_mod(x_ref[...], g_ref[...], sh_ref[...], sc_ref[...]).astype(BF16)


def _normmod(x2, gain, mod3, seq, shift_idx, scale_idx, tm=512):
    m, d = x2.shape
    return pl.pallas_call(
        _normmod_kernel,
        grid=(m // tm,),
        in_specs=[pl.BlockSpec((tm, d), lambda i: (i, 0)),
                  pl.BlockSpec((1, d), lambda i: (0, 0)),
                  pl.BlockSpec((None, 1, d), lambda i: ((i * tm) // seq * 6 + shift_idx, 0, 0)),
                  pl.BlockSpec((None, 1, d), lambda i: ((i * tm) // seq * 6 + scale_idx, 0, 0))],
        out_specs=pl.BlockSpec((tm, d), lambda i: (i, 0)),
        out_shape=jax.ShapeDtypeStruct((m, d), BF16),
        compiler_params=_cparams(("arbitrary",)),
        name="normmod",
    )(x2, gain.reshape(1, d), mod3, mod3)


def _in_proj_kernel(h_ref, w_ref, o_ref):
    o_ref[...] = jnp.dot(h_ref[...], w_ref[...], preferred_element_type=F32)


def _in_proj(h, w, tm=1024, tn=512):
    m, d = h.shape
    tm = min(tm, m)
    n = w.shape[1]
    return pl.pallas_call(
        _in_proj_kernel,
        grid=(m // tm, n // tn),
        in_specs=[pl.BlockSpec((tm, d), lambda i, j: (i, 0)),
                  pl.BlockSpec((d, tn), lambda i, j: (0, j))],
        out_specs=pl.BlockSpec((tm, tn), lambda i, j: (i, j)),
        out_shape=jax.ShapeDtypeStruct((m, n), F32),
        compiler_params=_cparams(("arbitrary", "arbitrary")),
        name="in_proj",
    )(h, w)


def _ffn_up_kernel(h_ref, wg_ref, wu_ref, o_ref, sg_ref, su_ref):
    @pl.when(pl.program_id(1) == 0)
    def _():
        _cast_block(wg_ref, sg_ref)
        _cast_block(wu_ref, su_ref)

    h = h_ref[...]
    gate = jnp.dot(h, sg_ref[...], preferred_element_type=F32)
    up = jnp.dot(h, su_ref[...], preferred_element_type=F32)
    o_ref[...] = (jax.nn.silu(gate) * up).astype(BF16)


def _ffn_up(h, wg, wu, tm=1024, tn=512):
    m, d = h.shape
    tm = min(tm, m)
    n = wg.shape[1]
    return pl.pallas_call(
        _ffn_up_kernel,
        grid=(n // tn, m // tm),
        in_specs=[pl.BlockSpec((tm, d), lambda j, i: (i, 0)),
                  pl.BlockSpec((d, tn), lambda j, i: (0, j)),
                  pl.BlockSpec((d, tn), lambda j, i: (0, j))],
        out_specs=pl.BlockSpec((tm, tn), lambda j, i: (i, j)),
        out_shape=jax.ShapeDtypeStruct((m, n), BF16),
        scratch_shapes=[pltpu.VMEM((d, tn), BF16), pltpu.VMEM((d, tn), BF16)],
        compiler_params=_cparams(("arbitrary", "arbitrary")),
        name="ffn_up",
    )(h, wg, wu)


def _mm_res_kernel(*refs, n_lhs):
    lhs = refs[:n_lhs]
    ws = refs[n_lhs:2 * n_lhs]
    res_ref, gate_ref, o_ref = refs[2 * n_lhs:2 * n_lhs + 3]
    scr = refs[2 * n_lhs + 3:]

    @pl.when(pl.program_id(1) == 0)
    def _():
        for w_ref, s_ref in zip(ws, scr):
            _cast_block(w_ref, s_ref)

    acc = None
    for l_ref, s_ref in zip(lhs, scr):
        d = jnp.dot(l_ref[...], s_ref[...], preferred_element_type=F32)
        acc = d if acc is None else acc + d
    o_ref[...] = res_ref[...] + gate_ref[...] * acc


def _mm_res(lhs_list, w, res, mod3, seq, gate_idx, tm=512, tn=512):
    m, n = res.shape
    k = lhs_list[0].shape[1]
    n_lhs = len(lhs_list)
    in_specs = [pl.BlockSpec((tm, k), lambda j, i: (i, 0)) for _ in lhs_list]
    in_specs += [pl.BlockSpec((k, tn), functools.partial(lambda j, i, li: (li, j), li=li))
                 for li in range(n_lhs)]
    in_specs += [pl.BlockSpec((tm, tn), lambda j, i: (i, j)),
                 pl.BlockSpec((None, 1, tn), lambda j, i: ((i * tm) // seq * 6 + gate_idx, 0, j))]
    return pl.pallas_call(
        functools.partial(_mm_res_kernel, n_lhs=n_lhs),
        grid=(n // tn, m // tm),
        in_specs=in_specs,
        out_specs=pl.BlockSpec((tm, tn), lambda j, i: (i, j)),
        out_shape=jax.ShapeDtypeStruct((m, n), F32),
        scratch_shapes=[pltpu.VMEM((k, tn), BF16) for _ in lhs_list],
        compiler_params=_cparams(("arbitrary", "arbitrary")),
        name="mm_res",
    )(*lhs_list, *([w] * n_lhs), res, mod3)


def _rwkv_kernel(*refs, chunk, width, vres):
    it = iter(refs)
    p_ref, mu_ref, w0_ref, w2_ref, a0_ref, a2_ref, g2_ref = (next(it) for _ in range(7))
    if vres:
        v0_ref, v2_ref, vf_ref = next(it), next(it), next(it)
    kk_ref, ka_ref, rk_ref, lnw_ref, lnb_ref = (next(it) for _ in range(5))
    o_ref = next(it)
    vf_out_ref = None if vres else next(it)
    carry_scr, state_scr = next(it), next(it)

    L, C = chunk, width
    P2 = 2 * L
    pairs = range(C // LANES)
    ci = pl.program_id(1)

    @pl.when(ci == 0)
    def _():
        carry_scr[...] = jnp.zeros_like(carry_scr)
        state_scr[...] = jnp.zeros_like(state_scr)

    def shifted(lo, hi):
        p = p_ref[:, lo:hi]
        rolled = pltpu.roll(p, 1, 0)
        row = lax.broadcasted_iota(jnp.int32, p.shape, 0)
        prev = jnp.where(row == 0, carry_scr[SUBLANES - 1:SUBLANES, lo:hi], rolled)
        return p + (prev - p) * mu_ref[:, lo:hi]

    zl = shifted(3 * C, 3 * C + LORA_WINDOW)
    wl = w0_ref[...] + _bdot(jnp.tanh(zl), w2_ref[...])
    logdecay = -jnp.exp(-jax.nn.softplus(-wl) - 0.5)
    a_all = jax.nn.sigmoid(a0_ref[...] + _bdot(zl, a2_ref[...]))
    g_all = _bdot(jax.nn.sigmoid(zl), g2_ref[...])
    if vres:
        vmix = jax.nn.sigmoid(v0_ref[...] + _bdot(zl, v2_ref[...]))

    tr = lax.broadcasted_iota(jnp.int32, (L, L), 0)
    tc = lax.broadcasted_iota(jnp.int32, (L, L), 1)
    tri = (tc <= tr).astype(BF16)
    cum = _dot_ex(tri, logdecay, 3)
    p_inc = jnp.exp(cum)
    p_exc = jnp.exp(cum - logdecay)
    p_inv = jnp.exp(-cum)

    rr = lax.broadcasted_iota(jnp.int32, (P2, P2), 0)
    cc = lax.broadcasted_iota(jnp.int32, (P2, P2), 1)
    strict = cc < rr
    incl = cc <= rr
    eye = (cc == rr).astype(F32)
    lane = lax.broadcasted_iota(jnp.int32, (L, LANES), 1)
    head0 = lane < RWKV_HEAD_DIM
    seg = ((rr // RWKV_HEAD_DIM) == (cc // RWKV_HEAD_DIM)).astype(BF16)

    def expand(a):
        return jnp.concatenate([jnp.where(head0, a, 0.0), jnp.where(head0, 0.0, a)], axis=0)

    sls = [slice(hp * LANES, (hp + 1) * LANES) for hp in pairs]
    r_l, k_l, v_l, kk0_l, ssq_l = [], [], [], [], []
    for hp in pairs:
        sl = sls[hp]
        r_l.append(shifted(hp * LANES, (hp + 1) * LANES))
        k_l.append(shifted(C + hp * LANES, C + (hp + 1) * LANES))
        v = shifted(2 * C + hp * LANES, 2 * C + (hp + 1) * LANES)
        if vres:
            v = v + (vf_ref[:, sl] - v) * vmix[:, sl]
        else:
            vf_out_ref[:, sl] = v
        v_l.append(v)
        kk0 = k_l[hp] * kk_ref[:, sl]
        kk0_l.append(kk0)
        ssq_l.append(_dot_xe(kk0 * kk0, seg, 2))

    lhs_l, rhs_l, ax_l, vx_l, k2_l = [], [], [], [], []
    for hp in pairs:
        sl = sls[hp]
        a = a_all[:, sl]
        kk = kk0_l[hp] * lax.rsqrt(jnp.maximum(ssq_l[hp], 1e-24))
        k2 = k_l[hp] * (1.0 + (a - 1.0) * ka_ref[:, sl])
        k2_l.append(k2)
        ax = expand(-kk * p_exc[:, sl]).astype(BF16)
        rx = expand(r_l[hp] * p_inc[:, sl]).astype(BF16)
        bx = expand(kk * a * p_inv[:, sl]).astype(BF16)
        kx = expand(k2 * p_inv[:, sl]).astype(BF16)
        ax_l.append(ax)
        vx_l.append(expand(v_l[hp]).astype(BF16))
        lhs_l.append(jnp.concatenate([ax, rx], axis=0))
        rhs_l.append(jnp.concatenate([bx, kx], axis=0))

    gram_l = [_bdot_nt(lhs_l[hp], rhs_l[hp]) for hp in pairs]
    a_ak_l = [jnp.where(strict, g[:P2, P2:], 0.0).astype(BF16) for g in gram_l]
    a_r_l = [jnp.concatenate([jnp.where(incl, g[P2:, :P2], 0.0),
                              jnp.where(incl, g[P2:, P2:], 0.0)], axis=1).astype(BF16) for g in gram_l]

    npow_l = [jnp.where(strict, g[:P2, :P2], 0.0) for g in gram_l]
    tinv_l = [eye + n for n in npow_l]
    span = 2
    while span < L:
        npow_l = [_bdot(n, n) for n in npow_l]
        tinv_l = [t + _bdot(t, n) for t, n in zip(tinv_l, npow_l)]
        span *= 2

    akv_l = [_bdot(a_ak_l[hp], vx_l[hp]) for hp in pairs]
    w_l = [_bdot(tinv_l[hp], jnp.concatenate([ax_l[hp], akv_l[hp].astype(BF16)], axis=1)) for hp in pairs]

    s0_l = [state_scr[hp] for hp in pairs]
    z_l = [_bdot_nt(jnp.concatenate([w_l[hp][:, :LANES].astype(BF16), lhs_l[hp][P2:]], axis=0), s0_l[hp])
           for hp in pairs]
    uv_l = [jnp.concatenate([(z_l[hp][:P2] + w_l[hp][:, LANES:]).astype(BF16), vx_l[hp]], axis=0)
            for hp in pairs]
    y2_l = [z_l[hp][P2:] + _bdot(a_r_l[hp], uv_l[hp]) for hp in pairs]
    for hp in pairs:
        s_new = s0_l[hp] + _bdot_tn(uv_l[hp], rhs_l[hp])
        state_scr[hp] = s_new * p_inc[L - 1:L, sls[hp]]

    y_l = [y2[:L] + y2[L:] for y2 in y2_l]
    mean_l = [_dot_xe(y, seg, 2) * (1.0 / RWKV_HEAD_DIM) for y in y_l]
    dev_l = [y - mu for y, mu in zip(y_l, mean_l)]
    var_l = [_dot_xe(d * d, seg, 2) * (1.0 / RWKV_HEAD_DIM) for d in dev_l]
    bon_l = [_dot_xe(r_l[hp] * k2_l[hp] * rk_ref[:, sls[hp]], seg, 2) for hp in pairs]
    for hp in pairs:
        sl = sls[hp]
        yn = dev_l[hp] * lax.rsqrt(var_l[hp] + GN_EPS) * lnw_ref[:, sl] + lnb_ref[:, sl]
        o_ref[:, sl] = ((yn + bon_l[hp] * v_l[hp]) * g_all[:, sl]).astype(o_ref.dtype)

    carry_scr[...] = p_ref[L - SUBLANES:L, :]


def _rwkv(proj, batch, seq, lp, v_first):
    vres = v_first is not None
    C = lp["w0"].shape[0]
    L = RWKV_CHUNK
    W = 3 * C + LORA_WINDOW
    row = lambda a: a.reshape(1, -1)
    full = lambda a: pl.BlockSpec(a.shape, lambda b, c: (0, 0))
    tok = pl.BlockSpec((L, C), lambda b, c: (b * (seq // L) + c, 0))
    args = [proj, lp["mu_pad"], row(lp["w0"]), lp["w2_pad"], row(lp["a0"]), lp["a2_pad"], lp["g2_pad"]]
    in_specs = [pl.BlockSpec((L, W), lambda b, c: (b * (seq // L) + c, 0))]
    in_specs += [full(a) for a in args[1:]]
    if vres:
        extra = [row(lp["v0"]), lp["v2_pad"]]
        args += extra + [v_first]
        in_specs += [full(a) for a in extra] + [tok]
    tail = [row(lp["k_k"]), row(lp["k_a"]), row(lp["r_k"]), row(lp["ln_w"]), row(lp["ln_b"])]
    args += tail
    in_specs += [full(a) for a in tail]
    y_sd = jax.ShapeDtypeStruct((batch * seq, C), BF16)
    v_sd = jax.ShapeDtypeStruct((batch * seq, C), F32)
    out = pl.pallas_call(
        functools.partial(_rwkv_kernel, chunk=L, width=C, vres=vres),
        grid=(batch, seq // L),
        in_specs=in_specs,
        out_specs=tok if vres else (tok, tok),
        out_shape=y_sd if vres else (y_sd, v_sd),
        scratch_shapes=[pltpu.VMEM((SUBLANES, W), F32),
                        pltpu.VMEM((C // LANES, LANES, LANES), F32)],
        compiler_params=_cparams(("arbitrary", "arbitrary")),
        name="rwkv7",
    )(*args)
    return (out, v_first) if vres else out


def _moba_kernel(q_ref, k_ref, v_ref, slope_ref, gain_ref, o_ref, *, n_blocks):
    BS = MOBA_BLOCK
    T = n_blocks * BS
    scale = MOBA_HEAD_DIM ** -0.5

    means = [jnp.mean(k_ref[j * BS:(j + 1) * BS, :], axis=0, keepdims=True) for j in range(n_blocks)]
    km = jnp.concatenate(means + [jnp.zeros((LANES - n_blocks, LANES), F32)], axis=0)
    gate_t = _dot3(km, q_ref[...], nt=True)[0:SUBLANES, :]
    blk = lax.broadcasted_iota(jnp.int32, (SUBLANES, T), 0)
    own_of = lax.broadcasted_iota(jnp.int32, (SUBLANES, T), 1) // BS
    past = blk < own_of
    gate_t = jnp.where(past, gate_t, -jnp.inf)
    beaten = jnp.zeros((SUBLANES, T), jnp.int32)
    for j in range(n_blocks):
        gj = gate_t[j:j + 1, :]
        ahead = (gj > gate_t) | ((gj == gate_t) & (blk > j))
        beaten = beaten + ahead.astype(jnp.int32)
    sel_t = ((beaten < MOBA_TOPK) & past).astype(F32)

    ind = (lax.broadcasted_iota(jnp.int32, (LANES, T), 1) // BS
           == lax.broadcasted_iota(jnp.int32, (LANES, T), 0)).astype(BF16)
    slope = slope_ref[:, 0:1]
    bias = slope * lax.broadcasted_iota(jnp.int32, (1, T), 1).astype(F32)
    causal = (lax.broadcasted_iota(jnp.int32, (BS, BS), 1)
              <= lax.broadcasted_iota(jnp.int32, (BS, BS), 0))
    zpad = jnp.zeros((LANES - SUBLANES, BS), F32)

    for own in range(n_blocks):
        lo, hi = own * BS, (own + 1) * BS
        q = q_ref[lo:hi, :].astype(BF16)
        s_o = _bdot_nt(q, k_ref[lo:hi, :]) * scale + bias[:, 0:BS]
        s_o = jnp.where(causal, s_o, -jnp.inf)
        m = jnp.max(s_o, axis=-1, keepdims=True)
        if own > 0:
            sel = jnp.concatenate([sel_t[:, lo:hi], zpad], axis=0).T.astype(BF16)
            picked = jnp.dot(sel, ind[:, 0:lo], preferred_element_type=F32) > 0.5
            s_p = _bdot_nt(q, k_ref[0:lo, :]) * scale + (bias[:, 0:lo] - slope * float(lo))
            s_p = jnp.where(picked, s_p, -jnp.inf)
            m = jnp.maximum(m, jnp.max(s_p, axis=-1, keepdims=True))
            e_p = jnp.exp(s_p - m)
        e_o = jnp.exp(s_o - m)
        denom = jnp.sum(e_o, axis=-1, keepdims=True)
        out = _bdot(e_o, v_ref[lo:hi, :])
        if own > 0:
            denom = denom + jnp.sum(e_p, axis=-1, keepdims=True)
            out = out + _bdot(e_p, v_ref[0:lo, :])
        out = out / denom
        out = out * lax.rsqrt(jnp.mean(out * out, axis=-1, keepdims=True) + RMS_EPS)
        o_ref[lo:hi, :] = (out * gain_ref[...]).astype(o_ref.dtype)


def _moba(proj, batch, seq, q_off, gain):
    width = gain.shape[0]
    heads = width // MOBA_HEAD_DIM
    nb = seq // MOBA_BLOCK
    assert nb <= SUBLANES and seq % MOBA_BLOCK == 0
    qo = q_off // LANES
    slopes = jnp.exp2(-8.0 * jnp.arange(1, heads + 1, dtype=F32) / heads)
    slopes = jnp.broadcast_to(slopes[:, None, None], (heads, 1, LANES))
    return pl.pallas_call(
        functools.partial(_moba_kernel, n_blocks=nb),
        grid=(batch, heads),
        in_specs=[pl.BlockSpec((seq, LANES), lambda b, h: (b, qo + h)),
                  pl.BlockSpec((seq, LANES), lambda b, h: (b, qo + heads + h)),
                  pl.BlockSpec((seq, LANES), lambda b, h: (b, qo + 2 * heads + h)),
                  pl.BlockSpec((None, 1, LANES), lambda b, h: (h, 0, 0)),
                  pl.BlockSpec((1, LANES), lambda b, h: (0, h))],
        out_specs=pl.BlockSpec((seq, LANES), lambda b, h: (b, h)),
        out_shape=jax.ShapeDtypeStruct((batch * seq, width), BF16),
        compiler_params=_cparams(("arbitrary", "arbitrary")),
        name="moba",
    )(proj, proj, proj, slopes, gain.reshape(1, width))


def _router_kernel(x_ref, g_ref, sh_ref, sc_ref, wr_ref, h_ref, route_ref, *, n_experts):
    h = _norm_mod(x_ref[...], g_ref[...], sh_ref[...], sc_ref[...])
    h_ref[...] = h.astype(BF16)
    logits = _dot3(h, wr_ref[...])
    col = lax.broadcasted_iota(jnp.int32, logits.shape, 1)
    neg = -jnp.inf
    logits = jnp.where(col < n_experts, logits, neg)
    m1 = jnp.max(logits, axis=-1, keepdims=True)
    i1 = jnp.min(jnp.where(logits == m1, col, LANES), axis=-1, keepdims=True)
    rest = jnp.where(col == i1, neg, logits)
    m2 = jnp.max(rest, axis=-1, keepdims=True)
    i2 = jnp.min(jnp.where(rest == m2, col, LANES), axis=-1, keepdims=True)
    e2 = jnp.exp(m2 - m1)
    wa = 1.0 / (1.0 + e2)
    wb = e2 / (1.0 + e2)
    route = jnp.where(col == 0, i1.astype(F32), 0.0)
    route = jnp.where(col == 1, i2.astype(F32), route)
    route = jnp.where(col == 2, wa, route)
    route = jnp.where(col == 3, wb, route)
    route_ref[...] = route


def _router(x2, gain, mod3, router_pad, seq, shift_idx, scale_idx, n_experts, tm=512):
    m, d = x2.shape
    sh = pl.BlockSpec((None, 1, d), lambda i: ((i * tm) // seq * 6 + shift_idx, 0, 0))
    sc = pl.BlockSpec((None, 1, d), lambda i: ((i * tm) // seq * 6 + scale_idx, 0, 0))
    return pl.pallas_call(
        functools.partial(_router_kernel, n_experts=n_experts),
        grid=(m // tm,),
        in_specs=[pl.BlockSpec((tm, d), lambda i: (i, 0)),
                  pl.BlockSpec((1, d), lambda i: (0, 0)),
                  sh, sc,
                  pl.BlockSpec((d, LANES), lambda i: (0, 0))],
        out_specs=(pl.BlockSpec((tm, d), lambda i: (i, 0)),
                   pl.BlockSpec((tm, LANES), lambda i: (i, 0))),
        out_shape=(jax.ShapeDtypeStruct((m, d), BF16),
                   jax.ShapeDtypeStruct((m, LANES), F32)),
        compiler_params=_cparams(("arbitrary",)),
        name="router",
    )(x2, gain.reshape(1, d), mod3, mod3, router_pad)


def _moe_up_kernel(te_ref, tv_ref, tn_ref, x_ref, wg_ref, wu_ref, o_ref, sg_ref, su_ref):
    i = pl.program_id(1)

    @pl.when(tn_ref[i] > 0)
    def _():
        _cast_block(wg_ref, sg_ref)
        _cast_block(wu_ref, su_ref)

    @pl.when(tv_ref[i] > 0)
    def _():
        x = x_ref[...]
        gate = jnp.dot(x, sg_ref[...], preferred_element_type=F32)
        up = jnp.dot(x, su_ref[...], preferred_element_type=F32)
        o_ref[...] = (jax.nn.silu(gate) * up).astype(BF16)

    @pl.when(tv_ref[i] == 0)
    def _():
        o_ref[...] = jnp.zeros_like(o_ref)


def _moe_up(tiles, xs, wg, wu, tm, tf=512):
    p, d = xs.shape
    f = wg.shape[2]
    grid_spec = pltpu.PrefetchScalarGridSpec(
        num_scalar_prefetch=3,
        grid=(f // tf, p // tm),
        in_specs=[pl.BlockSpec((tm, d), lambda j, i, te, tv, tn: (i, 0)),
                  pl.BlockSpec((None, d, tf), lambda j, i, te, tv, tn: (te[i], 0, j)),
                  pl.BlockSpec((None, d, tf), lambda j, i, te, tv, tn: (te[i], 0, j))],
        out_specs=pl.BlockSpec((tm, tf), lambda j, i, te, tv, tn: (i, j)),
        scratch_shapes=[pltpu.VMEM((d, tf), BF16), pltpu.VMEM((d, tf), BF16)],
    )
    return pl.pallas_call(
        _moe_up_kernel,
        grid_spec=grid_spec,
        out_shape=jax.ShapeDtypeStruct((p, f), BF16),
        compiler_params=_cparams(("arbitrary", "arbitrary")),
        name="moe_up",
    )(*tiles, xs, wg, wu)


def _moe_down_kernel(te_ref, tv_ref, tn_ref, a_ref, wd_ref, o_ref, sd_ref):
    i = pl.program_id(1)

    @pl.when(tn_ref[i] > 0)
    def _():
        _cast_block(wd_ref, sd_ref)

    @pl.when(tv_ref[i] > 0)
    def _():
        o_ref[...] = jnp.dot(a_ref[...], sd_ref[...], preferred_element_type=F32)

    @pl.when(tv_ref[i] == 0)
    def _():
        o_ref[...] = jnp.zeros_like(o_ref)


def _moe_down(tiles, act, wd, tm, tn=512):
    p, f = act.shape
    n = wd.shape[2]
    grid_spec = pltpu.PrefetchScalarGridSpec(
        num_scalar_prefetch=3,
        grid=(n // tn, p // tm),
        in_specs=[pl.BlockSpec((tm, f), lambda j, i, te, tv, tn_: (i, 0)),
                  pl.BlockSpec((None, f, tn), lambda j, i, te, tv, tn_: (te[i], 0, j))],
        out_specs=pl.BlockSpec((tm, tn), lambda j, i, te, tv, tn_: (i, j)),
        scratch_shapes=[pltpu.VMEM((f, tn), BF16)],
    )
    return pl.pallas_call(
        _moe_down_kernel,
        grid_spec=grid_spec,
        out_shape=jax.ShapeDtypeStruct((p, n), F32),
        compiler_params=_cparams(("arbitrary", "arbitrary")),
        name="moe_down",
    )(*tiles, act, wd)


def _moe_combine_kernel(x_ref, y_ref, route_ref, gate_ref, *rest, final):
    d = x_ref.shape[1]
    route = route_ref[...]
    f = route[:, 2:3] * y_ref[:, 0:d] + route[:, 3:4] * y_ref[:, d:2 * d]
    x = x_ref[...] + gate_ref[...] * f
    if final:
        g_ref, o_ref = rest
        x = x * lax.rsqrt(jnp.mean(x * x, axis=-1, keepdims=True) + RMS_EPS) * g_ref[...]
    else:
        o_ref, = rest
    o_ref[...] = x


def _moe_combine(x2, yab, route, mod3, seq, gate_idx, final_gain, tm=256):
    m, d = x2.shape
    final = final_gain is not None
    in_specs = [pl.BlockSpec((tm, d), lambda i: (i, 0)),
                pl.BlockSpec((tm, TOP_K * d), lambda i: (i, 0)),
                pl.BlockSpec((tm, LANES), lambda i: (i, 0)),
                pl.BlockSpec((None, 1, d), lambda i: ((i * tm) // seq * 6 + gate_idx, 0, 0))]
    args = [x2, yab, route, mod3]
    if final:
        in_specs.append(pl.BlockSpec((1, d), lambda i: (0, 0)))
        args.append(final_gain.reshape(1, d))
    return pl.pallas_call(
        functools.partial(_moe_combine_kernel, final=final),
        grid=(m // tm,),
        in_specs=in_specs,
        out_specs=pl.BlockSpec((tm, d), lambda i: (i, 0)),
        out_shape=jax.ShapeDtypeStruct((m, d), F32),
        compiler_params=_cparams(("arbitrary",)),
        name="moe_combine",
    )(*args)


def _moe(x2, gain, mod3, router, wg, wu, wd, seq, final_gain, tm=512):
    m, d = x2.shape
    n_experts = router.shape[1]
    router_pad = jnp.pad(router, ((0, 0), (0, LANES - n_experts)))
    h, route = _router(x2, gain, mod3, router_pad, seq, 3, 4, n_experts)

    slots = m * TOP_K
    n_tiles = slots // tm + n_experts
    flat_e = route[:, 0:TOP_K].astype(jnp.int32).reshape(slots)
    onehot = (flat_e[:, None] == jnp.arange(n_experts)[None, :]).astype(jnp.int32)
    rank = jnp.sum((jnp.cumsum(onehot, axis=0) - onehot) * onehot, axis=1)
    counts = jnp.sum(onehot, axis=0)
    padded = (counts + tm - 1) // tm * tm
    ends = jnp.cumsum(padded)
    dest = jnp.sum(onehot * (ends - padded)[None, :], axis=1) + rank
    sorted_tok = jnp.zeros((n_tiles * tm,), jnp.int32).at[dest].set(jnp.arange(slots, dtype=jnp.int32) // TOP_K)
    tile_start = jnp.arange(n_tiles, dtype=jnp.int32) * tm
    tile_expert = jnp.minimum(jnp.sum((tile_start[:, None] >= ends[None, :]).astype(jnp.int32), axis=1),
                              n_experts - 1)
    tile_valid = (tile_start < ends[-1]).astype(jnp.int32)
    tile_new = jnp.concatenate([jnp.ones((1,), jnp.int32),
                                (tile_expert[1:] != tile_expert[:-1]).astype(jnp.int32)])
    tiles = (tile_expert, tile_valid, tile_new)

    xs = jnp.take(h, sorted_tok, axis=0)
    act = _moe_up(tiles, xs, wg, wu, tm)
    ys = _moe_down(tiles, act, wd, tm)
    yab = jnp.take(ys, dest, axis=0).reshape(m, TOP_K * d)
    return _moe_combine(x2, yab, route, mod3, seq, 5, final_gain)


def _final_norm_kernel(x_ref, g_ref, o_ref):
    x = x_ref[...]
    o_ref[...] = x * lax.rsqrt(jnp.mean(x * x, axis=-1, keepdims=True) + RMS_EPS) * g_ref[...]


def _final_norm(x2, gain, tm=512):
    m, d = x2.shape
    return pl.pallas_call(
        _final_norm_kernel,
        grid=(m // tm,),
        in_specs=[pl.BlockSpec((tm, d), lambda i: (i, 0)),
                  pl.BlockSpec((1, d), lambda i: (0, 0))],
        out_specs=pl.BlockSpec((tm, d), lambda i: (i, 0)),
        out_shape=jax.ShapeDtypeStruct((m, d), F32),
        compiler_params=_cparams(("arbitrary",)),
        name="final_norm",
    )(x2, gain.reshape(1, d))


def _prep_mixer(lp, C):
    w_in, mu = lp["w_in"], lp["shift_mu"]
    n_r = mu.shape[0]
    assert 3 * C < n_r <= 3 * C + LORA_WINDOW <= RWKV_PROJ_COLS
    out = dict(lp)
    out["w_in_r"] = w_in[:, :RWKV_PROJ_COLS].astype(BF16)
    out["w_in_m"] = w_in[:, n_r:].astype(BF16)
    out["mu_pad"] = jnp.pad(mu, (0, 3 * C + LORA_WINDOW - n_r)).reshape(1, -1)
    off = 0
    for name in ("w2", "a2", "g2") + (("v2",) if "v2" in lp else ()):
        rank = lp[name].shape[0]
        out[name + "_pad"] = jnp.pad(lp[name], ((off, LORA_WINDOW - off - rank), (0, 0)))
        off += rank
    assert 3 * C + off == n_r
    return out


def _layer(x2, mod3, lp, batch, seq, v_first, final_gain):
    C = lp["w0"].shape[0]
    lp = _prep_mixer(lp, C)
    h = _normmod(x2, lp["norm_mix"], mod3, seq, 0, 1)
    y_r, v_first = _rwkv(_in_proj(h, lp["w_in_r"]), batch, seq, lp, v_first)
    y_m = _moba(_in_proj(h, lp["w_in_m"]), batch, seq, 0, lp["moba_gain"])
    x2 = _mm_res([y_r, y_m], lp["w_out"], x2, mod3, seq, 2)
    if "ffn_gate" in lp:
        act = _ffn_up(_normmod(x2, lp["norm_ffn"], mod3, seq, 3, 4), lp["ffn_gate"], lp["ffn_up"])
        x2 = _mm_res([act], lp["ffn_down"], x2, mod3, seq, 5)
        if final_gain is not None:
            x2 = _final_norm(x2, final_gain)
    else:
        x2 = _moe(x2, lp["norm_ffn"], mod3, lp["router"], lp["exp_gate"], lp["exp_up"],
                  lp["exp_down"], seq, final_gain)
    return x2, v_first


def _forward(x, c, layers, norm_out):
    batch, seq, d = x.shape
    x2 = x.reshape(batch * seq, d)
    c_pad = jnp.pad(c, ((0, SUBLANES - batch), (0, 0)))
    v_first = None
    for li, lp in enumerate(layers):
        mod = _adaln(c_pad, lp["mod_w"], lp["mod_b"])[:batch]
        mod3 = mod.reshape(batch * 6, 1, d)
        last = li == len(layers) - 1
        x2, v_first = _layer(x2, mod3, lp, batch, seq, v_first, norm_out if last else None)
    return x2.reshape(batch, seq, d)


def kernel(x, c, l0_mod_w, l0_mod_b, l0_norm_mix, l0_w_in, l0_shift_mu, l0_w0, l0_w2, l0_a0, l0_a2, l0_g2, l0_k_k, l0_k_a, l0_r_k, l0_ln_w, l0_ln_b, l0_moba_gain, l0_w_out, l0_norm_ffn, l0_ffn_gate, l0_ffn_up, l0_ffn_down, l1_mod_w, l1_mod_b, l1_norm_mix, l1_w_in, l1_shift_mu, l1_w0, l1_w2, l1_a0, l1_a2, l1_g2, l1_v0, l1_v2, l1_k_k, l1_k_a, l1_r_k, l1_ln_w, l1_ln_b, l1_moba_gain, l1_w_out, l1_norm_ffn, l1_router, l1_exp_gate, l1_exp_up, l1_exp_down, norm_out):
    layers = (
        dict(mod_w=l0_mod_w, mod_b=l0_mod_b, norm_mix=l0_norm_mix, w_in=l0_w_in, shift_mu=l0_shift_mu,
             w0=l0_w0, w2=l0_w2, a0=l0_a0, a2=l0_a2, g2=l0_g2, k_k=l0_k_k, k_a=l0_k_a, r_k=l0_r_k,
             ln_w=l0_ln_w, ln_b=l0_ln_b, moba_gain=l0_moba_gain, w_out=l0_w_out, norm_ffn=l0_norm_ffn,
             ffn_gate=l0_ffn_gate, ffn_up=l0_ffn_up, ffn_down=l0_ffn_down),
        dict(mod_w=l1_mod_w, mod_b=l1_mod_b, norm_mix=l1_norm_mix, w_in=l1_w_in, shift_mu=l1_shift_mu,
             w0=l1_w0, w2=l1_w2, a0=l1_a0, a2=l1_a2, g2=l1_g2, v0=l1_v0, v2=l1_v2, k_k=l1_k_k,
             k_a=l1_k_a, r_k=l1_r_k, ln_w=l1_ln_w, ln_b=l1_ln_b, moba_gain=l1_moba_gain, w_out=l1_w_out,
             norm_ffn=l1_norm_ffn, router=l1_router, exp_gate=l1_exp_gate, exp_up=l1_exp_up,
             exp_down=l1_exp_down),
    )
    return _forward(x, c, layers, norm_out)
```

```python
import functools

import jax
import jax.numpy as jnp
from jax import lax
from jax.experimental import pallas as pl
from jax.experimental.pallas import tpu as pltpu

F32 = jnp.float32
BF16 = jnp.bfloat16

RMS_EPS = 1e-6
GN_EPS = 64e-5
RWKV_HEAD_DIM = 64
MOBA_HEAD_DIM = 128
MOBA_BLOCK = 256
MOBA_TOPK = 3
TOP_K = 2
LANES = 128
SUBLANES = 8
RWKV_CHUNK = 64
VMEM_LIMIT = 52 * 1024 * 1024
CAST_ROWS = 256
MOE_SUB = 256

LORA_WINDOW = 384
RWKV_PROJ_COLS = 3584


def _cparams(sem):
    return pltpu.CompilerParams(dimension_semantics=sem, vmem_limit_bytes=VMEM_LIMIT)


def _bdot(a, b):
    return jnp.dot(a.astype(BF16), b.astype(BF16), preferred_element_type=F32)


def _bdot_nt(a, b):
    return lax.dot_general(a.astype(BF16), b.astype(BF16), (((1,), (1,)), ((), ())),
                           preferred_element_type=F32)


def _bdot_tn(a, b):
    return lax.dot_general(a.astype(BF16), b.astype(BF16), (((0,), (0,)), ((), ())),
                           preferred_element_type=F32)


def _split(x, terms):
    parts = []
    rem = x
    for _ in range(terms):
        hi = rem.astype(BF16)
        parts.append(hi)
        rem = rem - hi.astype(F32)
    return parts


def _dot_xe(x, e_bf16, terms):
    acc = None
    for part in _split(x, terms):
        d = jnp.dot(part, e_bf16, preferred_element_type=F32)
        acc = d if acc is None else acc + d
    return acc


def _dot_ex(e_bf16, x, terms):
    acc = None
    for part in _split(x, terms):
        d = jnp.dot(e_bf16, part, preferred_element_type=F32)
        acc = d if acc is None else acc + d
    return acc


def _dot3(a, b, nt=False):
    a_hi, a_lo = _split(a, 2)
    b_hi, b_lo = _split(b, 2)
    if nt:
        f = lambda u, v: lax.dot_general(u, v, (((1,), (1,)), ((), ())), preferred_element_type=F32)
    else:
        f = lambda u, v: jnp.dot(u, v, preferred_element_type=F32)
    return f(a_hi, b_hi) + (f(a_lo, b_hi) + f(a_hi, b_lo))


def _norm_mod(x, gain, shift, scale):
    y = x * lax.rsqrt(jnp.mean(x * x, axis=-1, keepdims=True) + RMS_EPS)
    return (y * gain) * (1.0 + scale) + shift


def _cast_block(w_ref, s_ref):
    if w_ref.dtype == BF16:
        return
    rows = w_ref.shape[0]
    step = CAST_ROWS if rows % CAST_ROWS == 0 else rows

    def body(r, carry):
        sl = pl.ds(pl.multiple_of(r * step, step), step)
        s_ref[sl, :] = w_ref[sl, :].astype(BF16)
        return carry

    lax.fori_loop(0, rows // step, body, 0)


def _mod_kernel(c_ref, w_ref, b_ref, o_ref):
    c = c_ref[...]
    o_ref[...] = _dot3(c * jax.nn.sigmoid(c), w_ref[...]) + b_ref[...]


def _adaln(c_pad, mod_w, mod_b, tn=512):
    m, d = c_pad.shape
    n = mod_w.shape[1]
    return pl.pallas_call(
        _mod_kernel,
        grid=(n // tn,),
        in_specs=[pl.BlockSpec((m, d), lambda j: (0, 0)),
                  pl.BlockSpec((d, tn), lambda j: (0, j)),
                  pl.BlockSpec((1, tn), lambda j: (0, j))],
        out_specs=pl.BlockSpec((m, tn), lambda j: (0, j)),
        out_shape=jax.ShapeDtypeStruct((m, n), F32),
        compiler_params=_cparams(("arbitrary",)),
        name="adaln",
    )(c_pad, mod_w, mod_b.reshape(1, n))


def _normmod_kernel(x_ref, g_ref, sh_ref, sc_ref, o_ref):
    o_ref[...] = _norm_mod(x_ref[...], g_ref[...], sh_ref[...], sc_ref[...]).astype(BF16)


def _normmod(x2, gain, mod3, seq, shift_idx, scale_idx, tm=512):
    m, d = x2.shape
    return pl.pallas_call(
        _normmod_kernel,
        grid=(m // tm,),
        in_specs=[pl.BlockSpec((tm, d), lambda i: (i, 0)),
                  pl.BlockSpec((1, d), lambda i: (0, 0)),
                  pl.BlockSpec((None, 1, d), lambda i: ((i * tm) // seq * 6 + shift_idx, 0, 0)),
                  pl.BlockSpec((None, 1, d), lambda i: ((i * tm) // seq * 6 + scale_idx, 0, 0))],
        out_specs=pl.BlockSpec((tm, d), lambda i: (i, 0)),
        out_shape=jax.ShapeDtypeStruct((m, d), BF16),
        compiler_params=_cparams(("arbitrary",)),
        name="normmod",
    )(x2, gain.reshape(1, d), mod3, mod3)


def _in_proj_kernel(h_ref, w_ref, o_ref, s_ref):
    @pl.when(pl.program_id(1) == 0)
    def _():
        s_ref[...] = w_ref[...].astype(BF16)

    o_ref[...] = jnp.dot(h_ref[...], s_ref[...], preferred_element_type=F32)


def _in_proj(h, w, n, tm=1024, tn=512):
    m, d = h.shape
    tm = min(tm, m)
    return pl.pallas_call(
        _in_proj_kernel,
        grid=(n // tn, m // tm),
        in_specs=[pl.BlockSpec((tm, d), lambda j, i: (i, 0)),
                  pl.BlockSpec((d, tn), lambda j, i: (0, j))],
        out_specs=pl.BlockSpec((tm, tn), lambda j, i: (i, j)),
        out_shape=jax.ShapeDtypeStruct((m, n), F32),
        scratch_shapes=[pltpu.VMEM((d, tn), BF16)],
        compiler_params=_cparams(("arbitrary", "arbitrary")),
        name="in_proj",
    )(h, w)


def _ffn_up_kernel(h_ref, wg_ref, wu_ref, o_ref, sg_ref, su_ref):
    @pl.when(pl.program_id(1) == 0)
    def _():
        _cast_block(wg_ref, sg_ref)
        _cast_block(wu_ref, su_ref)

    h = h_ref[...]
    gate = jnp.dot(h, sg_ref[...], preferred_element_type=F32)
    up = jnp.dot(h, su_ref[...], preferred_element_type=F32)
    o_ref[...] = (jax.nn.silu(gate) * up).astype(BF16)


def _ffn_up(h, wg, wu, tm=1024, tn=512):
    m, d = h.shape
    tm = min(tm, m)
    n = wg.shape[1]
    return pl.pallas_call(
        _ffn_up_kernel,
        grid=(n // tn, m // tm),
        in_specs=[pl.BlockSpec((tm, d), lambda j, i: (i, 0)),
                  pl.BlockSpec((d, tn), lambda j, i: (0, j)),
                  pl.BlockSpec((d, tn), lambda j, i: (0, j))],
        out_specs=pl.BlockSpec((tm, tn), lambda j, i: (i, j)),
        out_shape=jax.ShapeDtypeStruct((m, n), BF16),
        scratch_shapes=[pltpu.VMEM((d, tn), BF16), pltpu.VMEM((d, tn), BF16)],
        compiler_params=_cparams(("arbitrary", "arbitrary")),
        name="ffn_up",
    )(h, wg, wu)


def _mm_res_kernel(*refs, n_lhs):
    lhs = refs[:n_lhs]
    ws = refs[n_lhs:2 * n_lhs]
    res_ref, gate_ref, o_ref = refs[2 * n_lhs:2 * n_lhs + 3]
    scr = refs[2 * n_lhs + 3:]

    @pl.when(pl.program_id(1) == 0)
    def _():
        for w_ref, s_ref in zip(ws, scr):
            _cast_block(w_ref, s_ref)

    acc = None
    for l_ref, s_ref in zip(lhs, scr):
        d = jnp.dot(l_ref[...], s_ref[...], preferred_element_type=F32)
        acc = d if acc is None else acc + d
    o_ref[...] = res_ref[...] + gate_ref[...] * acc


def _mm_res(lhs_list, w, res, mod3, seq, gate_idx, tm=512, tn=512):
    m, n = res.shape
    tm = min(tm, m)
    k = lhs_list[0].shape[1]
    n_lhs = len(lhs_list)
    in_specs = [pl.BlockSpec((tm, k), lambda j, i: (i, 0)) for _ in lhs_list]
    in_specs += [pl.BlockSpec((k, tn), functools.partial(lambda j, i, li: (li, j), li=li))
                 for li in range(n_lhs)]
    in_specs += [pl.BlockSpec((tm, tn), lambda j, i: (i, j)),
                 pl.BlockSpec((None, 1, tn), lambda j, i: ((i * tm) // seq * 6 + gate_idx, 0, j))]
    return pl.pallas_call(
        functools.partial(_mm_res_kernel, n_lhs=n_lhs),
        grid=(n // tn, m // tm),
        in_specs=in_specs,
        out_specs=pl.BlockSpec((tm, tn), lambda j, i: (i, j)),
        out_shape=jax.ShapeDtypeStruct((m, n), F32),
        scratch_shapes=[pltpu.VMEM((k, tn), BF16) for _ in lhs_list],
        compiler_params=_cparams(("arbitrary", "arbitrary")),
        name="mm_res",
    )(*lhs_list, *([w] * n_lhs), res, mod3)


def _rwkv_kernel(*refs, chunk, width, vres):
    it = iter(refs)
    p_ref, mu_ref, w0_ref, w2_ref, a0_ref, a2_ref, g2_ref = (next(it) for _ in range(7))
    if vres:
        v0_ref, v2_ref, vf_ref = next(it), next(it), next(it)
    kk_ref, ka_ref, rk_ref, lnw_ref, lnb_ref = (next(it) for _ in range(5))
    o_ref = next(it)
    vf_out_ref = None if vres else next(it)
    carry_scr, state_scr = next(it), next(it)

    L, C = chunk, width
    P2 = 2 * L
    pairs = range(C // LANES)
    ci = pl.program_id(1)

    @pl.when(ci == 0)
    def _():
        carry_scr[...] = jnp.zeros_like(carry_scr)
        state_scr[...] = jnp.zeros_like(state_scr)

    def shifted(lo, hi):
        p = p_ref[:, lo:hi]
        rolled = pltpu.roll(p, 1, 0)
        row = lax.broadcasted_iota(jnp.int32, p.shape, 0)
        prev = jnp.where(row == 0, carry_scr[SUBLANES - 1:SUBLANES, lo:hi], rolled)
        return p + (prev - p) * mu_ref[:, lo:hi]

    zl = shifted(3 * C, 3 * C + LORA_WINDOW)
    wl = w0_ref[...] + _bdot(jnp.tanh(zl), w2_ref[...])
    logdecay = -jnp.exp(-jax.nn.softplus(-wl) - 0.5)
    a_all = jax.nn.sigmoid(a0_ref[...] + _bdot(zl, a2_ref[...]))
    g_all = _bdot(jax.nn.sigmoid(zl), g2_ref[...])
    if vres:
        vmix = jax.nn.sigmoid(v0_ref[...] + _bdot(zl, v2_ref[...]))

    tr = lax.broadcasted_iota(jnp.int32, (L, L), 0)
    tc = lax.broadcasted_iota(jnp.int32, (L, L), 1)
    tri = (tc <= tr).astype(BF16)
    cum = _dot_ex(tri, logdecay, 3)
    p_inc = jnp.exp(cum)
    p_exc = jnp.exp(cum - logdecay)
    p_inv = jnp.exp(-cum)

    rr = lax.broadcasted_iota(jnp.int32, (P2, P2), 0)
    cc = lax.broadcasted_iota(jnp.int32, (P2, P2), 1)
    strict = cc < rr
    incl = cc <= rr
    eye = (cc == rr).astype(F32)
    lane = lax.broadcasted_iota(jnp.int32, (L, LANES), 1)
    head0 = lane < RWKV_HEAD_DIM
    seg = ((rr // RWKV_HEAD_DIM) == (cc // RWKV_HEAD_DIM)).astype(BF16)

    def expand(a):
        return jnp.concatenate([jnp.where(head0, a, 0.0), jnp.where(head0, 0.0, a)], axis=0)

    sls = [slice(hp * LANES, (hp + 1) * LANES) for hp in pairs]
    r_l, k_l, v_l, kk0_l, ssq_l = [], [], [], [], []
    for hp in pairs:
        sl = sls[hp]
        r_l.append(shifted(hp * LANES, (hp + 1) * LANES))
        k_l.append(shifted(C + hp * LANES, C + (hp + 1) * LANES))
        v = shifted(2 * C + hp * LANES, 2 * C + (hp + 1) * LANES)
        if vres:
            v = v + (vf_ref[:, sl] - v) * vmix[:, sl]
        else:
            vf_out_ref[:, sl] = v
        v_l.append(v)
        kk0 = k_l[hp] * kk_ref[:, sl]
        kk0_l.append(kk0)
        ssq_l.append(_dot_xe(kk0 * kk0, seg, 2))

    lhs_l, rhs_l, ax_l, vx_l, k2_l = [], [], [], [], []
    for hp in pairs:
        sl = sls[hp]
        a = a_all[:, sl]
        kk = kk0_l[hp] * lax.rsqrt(jnp.maximum(ssq_l[hp], 1e-24))
        k2 = k_l[hp] * (1.0 + (a - 1.0) * ka_ref[:, sl])
        k2_l.append(k2)
        ax = expand(-kk * p_exc[:, sl]).astype(BF16)
        rx = expand(r_l[hp] * p_inc[:, sl]).astype(BF16)
        bx = expand(kk * a * p_inv[:, sl]).astype(BF16)
        kx = expand(k2 * p_inv[:, sl]).astype(BF16)
        ax_l.append(ax)
        vx_l.append(expand(v_l[hp]).astype(BF16))
        lhs_l.append(jnp.concatenate([ax, rx], axis=0))
        rhs_l.append(jnp.concatenate([bx, kx], axis=0))

    gram_l = [_bdot_nt(lhs_l[hp], rhs_l[hp]) for hp in pairs]
    a_ak_l = [jnp.where(strict, g[:P2, P2:], 0.0).astype(BF16) for g in gram_l]
    a_r_l = [jnp.concatenate([jnp.where(incl, g[P2:, :P2], 0.0),
                              jnp.where(incl, g[P2:, P2:], 0.0)], axis=1).astype(BF16) for g in gram_l]

    npow_l = [jnp.where(strict, g[:P2, :P2], 0.0) for g in gram_l]
    tinv_l = [eye + n for n in npow_l]
    span = 2
    while span < L:
        npow_l = [_bdot(n, n) for n in npow_l]
        tinv_l = [t + _bdot(t, n) for t, n in zip(tinv_l, npow_l)]
        span *= 2

    akv_l = [_bdot(a_ak_l[hp], vx_l[hp]) for hp in pairs]
    w_l = [_bdot(tinv_l[hp], jnp.concatenate([ax_l[hp], akv_l[hp].astype(BF16)], axis=1)) for hp in pairs]

    s0_l = [state_scr[hp] for hp in pairs]
    z_l = [_bdot_nt(jnp.concatenate([w_l[hp][:, :LANES].astype(BF16), lhs_l[hp][P2:]], axis=0), s0_l[hp])
           for hp in pairs]
    uv_l = [jnp.concatenate([(z_l[hp][:P2] + w_l[hp][:, LANES:]).astype(BF16), vx_l[hp]], axis=0)
            for hp in pairs]
    y2_l = [z_l[hp][P2:] + _bdot(a_r_l[hp], uv_l[hp]) for hp in pairs]
    for hp in pairs:
        s_new = s0_l[hp] + _bdot_tn(uv_l[hp], rhs_l[hp])
        state_scr[hp] = s_new * p_inc[L - 1:L, sls[hp]]

    y_l = [y2[:L] + y2[L:] for y2 in y2_l]
    mean_l = [_dot_xe(y, seg, 2) * (1.0 / RWKV_HEAD_DIM) for y in y_l]
    dev_l = [y - mu for y, mu in zip(y_l, mean_l)]
    var_l = [_dot_xe(d * d, seg, 2) * (1.0 / RWKV_HEAD_DIM) for d in dev_l]
    bon_l = [_dot_xe(r_l[hp] * k2_l[hp] * rk_ref[:, sls[hp]], seg, 2) for hp in pairs]
    for hp in pairs:
        sl = sls[hp]
        yn = dev_l[hp] * lax.rsqrt(var_l[hp] + GN_EPS) * lnw_ref[:, sl] + lnb_ref[:, sl]
        o_ref[:, sl] = ((yn + bon_l[hp] * v_l[hp]) * g_all[:, sl]).astype(o_ref.dtype)

    carry_scr[...] = p_ref[L - SUBLANES:L, :]


def _rwkv(proj, batch, seq, lp, v_first):
    vres = v_first is not None
    C = lp["w0"].shape[0]
    L = RWKV_CHUNK
    W = 3 * C + LORA_WINDOW
    row = lambda a: a.reshape(1, -1)
    full = lambda a: pl.BlockSpec(a.shape, lambda b, c: (0, 0))
    tok = pl.BlockSpec((L, C), lambda b, c: (b * (seq // L) + c, 0))
    args = [proj, lp["mu_pad"], row(lp["w0"]), lp["w2_pad"], row(lp["a0"]), lp["a2_pad"], lp["g2_pad"]]
    in_specs = [pl.BlockSpec((L, W), lambda b, c: (b * (seq // L) + c, 0))]
    in_specs += [full(a) for a in args[1:]]
    if vres:
        extra = [row(lp["v0"]), lp["v2_pad"]]
        args += extra + [v_first]
        in_specs += [full(a) for a in extra] + [tok]
    tail = [row(lp["k_k"]), row(lp["k_a"]), row(lp["r_k"]), row(lp["ln_w"]), row(lp["ln_b"])]
    args += tail
    in_specs += [full(a) for a in tail]
    y_sd = jax.ShapeDtypeStruct((batch * seq, C), BF16)
    v_sd = jax.ShapeDtypeStruct((batch * seq, C), F32)
    out = pl.pallas_call(
        functools.partial(_rwkv_kernel, chunk=L, width=C, vres=vres),
        grid=(batch, seq // L),
        in_specs=in_specs,
        out_specs=tok if vres else (tok, tok),
        out_shape=y_sd if vres else (y_sd, v_sd),
        scratch_shapes=[pltpu.VMEM((SUBLANES, W), F32),
                        pltpu.VMEM((C // LANES, LANES, LANES), F32)],
        compiler_params=_cparams(("arbitrary", "arbitrary")),
        name="rwkv7",
    )(*args)
    return (out, v_first) if vres else out


def _moba_kernel(q_ref, k_ref, v_ref, slope_ref, gain_ref, o_ref, *, n_blocks):
    BS = MOBA_BLOCK
    T = n_blocks * BS
    scale = MOBA_HEAD_DIM ** -0.5

    means = [jnp.mean(k_ref[j * BS:(j + 1) * BS, :], axis=0, keepdims=True) for j in range(n_blocks)]
    km = jnp.concatenate(means + [jnp.zeros((LANES - n_blocks, LANES), F32)], axis=0)
    gate_t = _dot3(km, q_ref[...], nt=True)[0:SUBLANES, :]
    blk = lax.broadcasted_iota(jnp.int32, (SUBLANES, T), 0)
    own_of = lax.broadcasted_iota(jnp.int32, (SUBLANES, T), 1) // BS
    past = blk < own_of
    gate_t = jnp.where(past, gate_t, -jnp.inf)
    beaten = jnp.zeros((SUBLANES, T), jnp.int32)
    for j in range(n_blocks):
        gj = gate_t[j:j + 1, :]
        ahead = (gj > gate_t) | ((gj == gate_t) & (blk > j))
        beaten = beaten + ahead.astype(jnp.int32)
    sel_t = ((beaten < MOBA_TOPK) & past).astype(F32)

    ind = (lax.broadcasted_iota(jnp.int32, (LANES, T), 1) // BS
           == lax.broadcasted_iota(jnp.int32, (LANES, T), 0)).astype(BF16)
    slope = slope_ref[:, 0:1]
    bias = slope * lax.broadcasted_iota(jnp.int32, (1, T), 1).astype(F32)
    causal = (lax.broadcasted_iota(jnp.int32, (BS, BS), 1)
              <= lax.broadcasted_iota(jnp.int32, (BS, BS), 0))
    zpad = jnp.zeros((LANES - SUBLANES, BS), F32)

    for own in range(n_blocks):
        lo, hi = own * BS, (own + 1) * BS
        q = q_ref[lo:hi, :].astype(BF16)
        s_o = _bdot_nt(q, k_ref[lo:hi, :]) * scale + bias[:, 0:BS]
        s_o = jnp.where(causal, s_o, -jnp.inf)
        m = jnp.max(s_o, axis=-1, keepdims=True)
        if own > 0:
            sel = jnp.concatenate([sel_t[:, lo:hi], zpad], axis=0).T.astype(BF16)
            picked = jnp.dot(sel, ind[:, 0:lo], preferred_element_type=F32) > 0.5
            s_p = _bdot_nt(q, k_ref[0:lo, :]) * scale + (bias[:, 0:lo] - slope * float(lo))
            s_p = jnp.where(picked, s_p, -jnp.inf)
            m = jnp.maximum(m, jnp.max(s_p, axis=-1, keepdims=True))
            e_p = jnp.exp(s_p - m)
        e_o = jnp.exp(s_o - m)
        denom = jnp.sum(e_o, axis=-1, keepdims=True)
        out = _bdot(e_o, v_ref[lo:hi, :])
        if own > 0:
            denom = denom + jnp.sum(e_p, axis=-1, keepdims=True)
            out = out + _bdot(e_p, v_ref[0:lo, :])
        out = out / denom
        out = out * lax.rsqrt(jnp.mean(out * out, axis=-1, keepdims=True) + RMS_EPS)
        o_ref[lo:hi, :] = (out * gain_ref[...]).astype(o_ref.dtype)


def _moba(proj, batch, seq, q_off, gain):
    width = gain.shape[0]
    heads = width // MOBA_HEAD_DIM
    nb = seq // MOBA_BLOCK
    assert nb <= SUBLANES and seq % MOBA_BLOCK == 0
    qo = q_off // LANES
    slopes = jnp.exp2(-8.0 * jnp.arange(1, heads + 1, dtype=F32) / heads)
    slopes = jnp.broadcast_to(slopes[:, None, None], (heads, 1, LANES))
    return pl.pallas_call(
        functools.partial(_moba_kernel, n_blocks=nb),
        grid=(batch, heads),
        in_specs=[pl.BlockSpec((seq, LANES), lambda b, h: (b, qo + h)),
                  pl.BlockSpec((seq, LANES), lambda b, h: (b, qo + heads + h)),
                  pl.BlockSpec((seq, LANES), lambda b, h: (b, qo + 2 * heads + h)),
                  pl.BlockSpec((None, 1, LANES), lambda b, h: (h, 0, 0)),
                  pl.BlockSpec((1, LANES), lambda b, h: (0, h))],
        out_specs=pl.BlockSpec((seq, LANES), lambda b, h: (b, h)),
        out_shape=jax.ShapeDtypeStruct((batch * seq, width), BF16),
        compiler_params=_cparams(("arbitrary", "arbitrary")),
        name="moba",
    )(proj, proj, proj, slopes, gain.reshape(1, width))


def _router_kernel(x_ref, g_ref, sh_ref, sc_ref, wr_ref, h_ref, route_ref, *, n_experts):
    h = _norm_mod(x_ref[...], g_ref[...], sh_ref[...], sc_ref[...])
    h_ref[...] = h
    logits = _dot3(h, wr_ref[...])
    col = lax.broadcasted_iota(jnp.int32, logits.shape, 1)
    neg = -jnp.inf
    logits = jnp.where(col < n_experts, logits, neg)
    m1 = jnp.max(logits, axis=-1, keepdims=True)
    i1 = jnp.min(jnp.where(logits == m1, col, LANES), axis=-1, keepdims=True)
    rest = jnp.where(col == i1, neg, logits)
    m2 = jnp.max(rest, axis=-1, keepdims=True)
    i2 = jnp.min(jnp.where(rest == m2, col, LANES), axis=-1, keepdims=True)
    e2 = jnp.exp(m2 - m1)
    wa = 1.0 / (1.0 + e2)
    wb = e2 / (1.0 + e2)
    route = jnp.where(col == 0, i1.astype(F32), 0.0)
    route = jnp.where(col == 1, i2.astype(F32), route)
    route = jnp.where(col == 2, wa, route)
    route = jnp.where(col == 3, wb, route)
    route_ref[...] = route


def _router(x2, gain, mod3, router_pad, seq, shift_idx, scale_idx, n_experts, tm=512):
    m, d = x2.shape
    sh = pl.BlockSpec((None, 1, d), lambda i: ((i * tm) // seq * 6 + shift_idx, 0, 0))
    sc = pl.BlockSpec((None, 1, d), lambda i: ((i * tm) // seq * 6 + scale_idx, 0, 0))
    return pl.pallas_call(
        functools.partial(_router_kernel, n_experts=n_experts),
        grid=(m // tm,),
        in_specs=[pl.BlockSpec((tm, d), lambda i: (i, 0)),
                  pl.BlockSpec((1, d), lambda i: (0, 0)),
                  sh, sc,
                  pl.BlockSpec((d, LANES), lambda i: (0, 0))],
        out_specs=(pl.BlockSpec((tm, d), lambda i: (i, 0)),
                   pl.BlockSpec((tm, LANES), lambda i: (i, 0))),
        out_shape=(jax.ShapeDtypeStruct((m, d), F32),
                   jax.ShapeDtypeStruct((m, LANES), F32)),
        compiler_params=_cparams(("arbitrary",)),
        name="router",
    )(x2, gain.reshape(1, d), mod3, mod3, router_pad)


def _sub_blocks(rows, o_ref, compute):
    for r0 in range(0, o_ref.shape[0], MOE_SUB):
        sl = slice(r0, r0 + MOE_SUB)

        @pl.when(rows > r0)
        def _(sl=sl):
            o_ref[sl, :] = compute(sl)

        @pl.when(rows <= r0)
        def _(sl=sl):
            o_ref[sl, :] = jnp.zeros((MOE_SUB, o_ref.shape[1]), o_ref.dtype)


def _moe_up_kernel(te_ref, tr_ref, tn_ref, x_ref, wg_ref, wu_ref, o_ref, sg_ref, su_ref):
    i = pl.program_id(1)

    @pl.when(tn_ref[i] > 0)
    def _():
        _cast_block(wg_ref, sg_ref)
        _cast_block(wu_ref, su_ref)

    def compute(sl):
        x = x_ref[sl, :].astype(BF16)
        gate = jnp.dot(x, sg_ref[...], preferred_element_type=F32)
        up = jnp.dot(x, su_ref[...], preferred_element_type=F32)
        return (jax.nn.silu(gate) * up).astype(BF16)

    _sub_blocks(tr_ref[i], o_ref, compute)


def _moe_up(tiles, xs, wg, wu, tm, tf=512):
    p, d = xs.shape
    f = wg.shape[2]
    grid_spec = pltpu.PrefetchScalarGridSpec(
        num_scalar_prefetch=3,
        grid=(f // tf, p // tm),
        in_specs=[pl.BlockSpec((tm, d), lambda j, i, te, tv, tn: (i, 0)),
                  pl.BlockSpec((None, d, tf), lambda j, i, te, tv, tn: (te[i], 0, j)),
                  pl.BlockSpec((None, d, tf), lambda j, i, te, tv, tn: (te[i], 0, j))],
        out_specs=pl.BlockSpec((tm, tf), lambda j, i, te, tv, tn: (i, j)),
        scratch_shapes=[pltpu.VMEM((d, tf), BF16), pltpu.VMEM((d, tf), BF16)],
    )
    return pl.pallas_call(
        _moe_up_kernel,
        grid_spec=grid_spec,
        out_shape=jax.ShapeDtypeStruct((p, f), BF16),
        compiler_params=_cparams(("arbitrary", "arbitrary")),
        name="moe_up",
    )(*tiles, xs, wg, wu)


def _moe_down_kernel(te_ref, tr_ref, tn_ref, a_ref, wd_ref, o_ref, sd_ref):
    i = pl.program_id(1)

    @pl.when(tn_ref[i] > 0)
    def _():
        _cast_block(wd_ref, sd_ref)

    _sub_blocks(tr_ref[i], o_ref,
                lambda sl: jnp.dot(a_ref[sl, :], sd_ref[...], preferred_element_type=F32))


def _moe_down(tiles, act, wd, tm, tn=512):
    p, f = act.shape
    n = wd.shape[2]
    grid_spec = pltpu.PrefetchScalarGridSpec(
        num_scalar_prefetch=3,
        grid=(n // tn, p // tm),
        in_specs=[pl.BlockSpec((tm, f), lambda j, i, te, tv, tn_: (i, 0)),
                  pl.BlockSpec((None, f, tn), lambda j, i, te, tv, tn_: (te[i], 0, j))],
        out_specs=pl.BlockSpec((tm, tn), lambda j, i, te, tv, tn_: (i, j)),
        scratch_shapes=[pltpu.VMEM((f, tn), BF16)],
    )
    return pl.pallas_call(
        _moe_down_kernel,
        grid_spec=grid_spec,
        out_shape=jax.ShapeDtypeStruct((p, n), F32),
        compiler_params=_cparams(("arbitrary", "arbitrary")),
        name="moe_down",
    )(*tiles, act, wd)


def _moe_combine_kernel(x_ref, ya_ref, yb_ref, route_ref, gate_ref, *rest, final):
    route = route_ref[...]
    f = route[:, 2:3] * ya_ref[...] + route[:, 3:4] * yb_ref[...]
    x = x_ref[...] + gate_ref[...] * f
    if final:
        g_ref, o_ref = rest
        x = x * lax.rsqrt(jnp.mean(x * x, axis=-1, keepdims=True) + RMS_EPS) * g_ref[...]
    else:
        o_ref, = rest
    o_ref[...] = x


def _moe_combine(x2, yab, route, mod3, seq, gate_idx, final_gain, tm=256):
    m, d = x2.shape
    final = final_gain is not None
    in_specs = [pl.BlockSpec((tm, d), lambda i: (i, 0)),
                pl.BlockSpec((tm, d), lambda i: (i, 0)),
                pl.BlockSpec((tm, d), lambda i: (i + m // tm, 0)),
                pl.BlockSpec((tm, LANES), lambda i: (i, 0)),
                pl.BlockSpec((None, 1, d), lambda i: ((i * tm) // seq * 6 + gate_idx, 0, 0))]
    args = [x2, yab, yab, route, mod3]
    if final:
        in_specs.append(pl.BlockSpec((1, d), lambda i: (0, 0)))
        args.append(final_gain.reshape(1, d))
    return pl.pallas_call(
        functools.partial(_moe_combine_kernel, final=final),
        grid=(m // tm,),
        in_specs=in_specs,
        out_specs=pl.BlockSpec((tm, d), lambda i: (i, 0)),
        out_shape=jax.ShapeDtypeStruct((m, d), F32),
        compiler_params=_cparams(("arbitrary",)),
        name="moe_combine",
    )(*args)


def _moe(x2, gain, mod3, router, wg, wu, wd, seq, final_gain, tm=512):
    m, d = x2.shape
    n_experts = router.shape[1]
    router_pad = jnp.pad(router, ((0, 0), (0, LANES - n_experts)))
    h, route = _router(x2, gain, mod3, router_pad, seq, 3, 4, n_experts)

    slots = m * TOP_K
    n_tiles = slots // tm + n_experts
    flat_e = route[:, 0:TOP_K].astype(jnp.int32).reshape(slots)
    onehot = (flat_e[:, None] == jnp.arange(n_experts)[None, :]).astype(jnp.int32)
    rank = jnp.sum((jnp.cumsum(onehot, axis=0) - onehot) * onehot, axis=1)
    counts = jnp.sum(onehot, axis=0)
    padded = (counts + tm - 1) // tm * tm
    ends = jnp.cumsum(padded)
    dest = jnp.sum(onehot * (ends - padded)[None, :], axis=1) + rank
    sorted_tok = jnp.zeros((n_tiles * tm,), jnp.int32).at[dest].set(jnp.arange(slots, dtype=jnp.int32) // TOP_K)
    tile_start = jnp.arange(n_tiles, dtype=jnp.int32) * tm
    tile_expert = jnp.minimum(jnp.sum((tile_start[:, None] >= ends[None, :]).astype(jnp.int32), axis=1),
                              n_experts - 1)
    tile_rows = jnp.clip((ends - padded + counts)[tile_expert] - tile_start, 0, tm).astype(jnp.int32)
    tile_new = jnp.concatenate([jnp.ones((1,), jnp.int32),
                                (tile_expert[1:] != tile_expert[:-1]).astype(jnp.int32)])
    tiles = (tile_expert, tile_rows, tile_new)

    xs = h.at[sorted_tok].get(mode="promise_in_bounds")
    act = _moe_up(tiles, xs, wg, wu, tm)
    ys = _moe_down(tiles, act, wd, tm)
    dest_slot_major = dest.reshape(m, TOP_K).T.reshape(slots)
    yab = ys.at[dest_slot_major].get(mode="promise_in_bounds")
    return _moe_combine(x2, yab, route, mod3, seq, 5, final_gain)


def _final_norm_kernel(x_ref, g_ref, o_ref):
    x = x_ref[...]
    o_ref[...] = x * lax.rsqrt(jnp.mean(x * x, axis=-1, keepdims=True) + RMS_EPS) * g_ref[...]


def _final_norm(x2, gain, tm=512):
    m, d = x2.shape
    return pl.pallas_call(
        _final_norm_kernel,
        grid=(m // tm,),
        in_specs=[pl.BlockSpec((tm, d), lambda i: (i, 0)),
                  pl.BlockSpec((1, d), lambda i: (0, 0))],
        out_specs=pl.BlockSpec((tm, d), lambda i: (i, 0)),
        out_shape=jax.ShapeDtypeStruct((m, d), F32),
        compiler_params=_cparams(("arbitrary",)),
        name="final_norm",
    )(x2, gain.reshape(1, d))


def _prep_mixer(lp, C):
    w_in, mu = lp["w_in"], lp["shift_mu"]
    n_r = mu.shape[0]
    assert 3 * C < n_r <= 3 * C + LORA_WINDOW <= RWKV_PROJ_COLS
    out = dict(lp)
    out["w_in_m"] = w_in[:, n_r:]
    out["mu_pad"] = jnp.pad(mu, (0, 3 * C + LORA_WINDOW - n_r)).reshape(1, -1)
    off = 0
    for name in ("w2", "a2", "g2") + (("v2",) if "v2" in lp else ()):
        rank = lp[name].shape[0]
        out[name + "_pad"] = jnp.pad(lp[name], ((off, LORA_WINDOW - off - rank), (0, 0)))
        off += rank
    assert 3 * C + off == n_r
    return out


def _layer(x2, mod3, lp, batch, seq, v_first, final_gain):
    C = lp["w0"].shape[0]
    lp = _prep_mixer(lp, C)
    h = _normmod(x2, lp["norm_mix"], mod3, seq, 0, 1)
    y_r, v_first = _rwkv(_in_proj(h, lp["w_in"], RWKV_PROJ_COLS), batch, seq, lp, v_first)
    w_m = lp["w_in_m"]
    y_m = _moba(_in_proj(h, w_m, w_m.shape[1], tn=1024), batch, seq, 0, lp["moba_gain"])
    x2 = _mm_res([y_r, y_m], lp["w_out"], x2, mod3, seq, 2, tm=1024, tn=1024)
    if "ffn_gate" in lp:
        act = _ffn_up(_normmod(x2, lp["norm_ffn"], mod3, seq, 3, 4), lp["ffn_gate"], lp["ffn_up"])
        x2 = _mm_res([act], lp["ffn_down"], x2, mod3, seq, 5)
        if final_gain is not None:
            x2 = _final_norm(x2, final_gain)
    else:
        x2 = _moe(x2, lp["norm_ffn"], mod3, lp["router"], lp["exp_gate"], lp["exp_up"],
                  lp["exp_down"], seq, final_gain)
    return x2, v_first


def _forward(x, c, layers, norm_out):
    batch, seq, d = x.shape
    x2 = x.reshape(batch * seq, d)
    c_pad = jnp.pad(c, ((0, SUBLANES - batch), (0, 0)))
    v_first = None
    for li, lp in enumerate(layers):
        mod = _adaln(c_pad, lp["mod_w"], lp["mod_b"])[:batch]
        mod3 = mod.reshape(batch * 6, 1, d)
        last = li == len(layers) - 1
        x2, v_first = _layer(x2, mod3, lp, batch, seq, v_first, norm_out if last else None)
    return x2.reshape(batch, seq, d)


def kernel(x, c, l0_mod_w, l0_mod_b, l0_norm_mix, l0_w_in, l0_shift_mu, l0_w0, l0_w2, l0_a0, l0_a2, l0_g2, l0_k_k, l0_k_a, l0_r_k, l0_ln_w, l0_ln_b, l0_moba_gain, l0_w_out, l0_norm_ffn, l0_ffn_gate, l0_ffn_up, l0_ffn_down, l1_mod_w, l1_mod_b, l1_norm_mix, l1_w_in, l1_shift_mu, l1_w0, l1_w2, l1_a0, l1_a2, l1_g2, l1_v0, l1_v2, l1_k_k, l1_k_a, l1_r_k, l1_ln_w, l1_ln_b, l1_moba_gain, l1_w_out, l1_norm_ffn, l1_router, l1_exp_gate, l1_exp_up, l1_exp_down, norm_out):
    layers = (
        dict(mod_w=l0_mod_w, mod_b=l0_mod_b, norm_mix=l0_norm_mix, w_in=l0_w_in, shift_mu=l0_shift_mu,
             w0=l0_w0, w2=l0_w2, a0=l0_a0, a2=l0_a2, g2=l0_g2, k_k=l0_k_k, k_a=l0_k_a, r_k=l0_r_k,
             ln_w=l0_ln_w, ln_b=l0_ln_b, moba_gain=l0_moba_gain, w_out=l0_w_out, norm_ffn=l0_norm_ffn,
             ffn_gate=l0_ffn_gate, ffn_up=l0_ffn_up, ffn_down=l0_ffn_down),
        dict(mod_w=l1_mod_w, mod_b=l1_mod_b, norm_mix=l1_norm_mix, w_in=l1_w_in, shift_mu=l1_shift_mu,
             w0=l1_w0, w2=l1_w2, a0=l1_a0, a2=l1_a2, g2=l1_g2, v0=l1_v0, v2=l1_v2, k_k=l1_k_k,
             k_a=l1_k_a, r_k=l1_r_k, ln_w=l1_ln_w, ln_b=l1_ln_b, moba_gain=l1_moba_gain, w_out=l1_w_out,
             norm_ffn=l1_norm_ffn, router=l1_router, exp_gate=l1_exp_gate, exp_up=l1_exp_up,
             exp_down=l1_exp_down),
    )
    return _forward(x, c, layers, norm_out)
```

```python
import functools

import jax
import jax.numpy as jnp
from jax import lax
from jax.experimental import pallas as pl
from jax.experimental.pallas import tpu as pltpu

F32 = jnp.float32
BF16 = jnp.bfloat16

RMS_EPS = 1e-6
GN_EPS = 64e-5
RWKV_HEAD_DIM = 64
MOBA_HEAD_DIM = 128
MOBA_BLOCK = 256
MOBA_TOPK = 3
TOP_K = 2
LANES = 128
SUBLANES = 8
RWKV_CHUNK = 64
VMEM_LIMIT = 52 * 1024 * 1024
CAST_ROWS = 256
MOE_SUB = 256
LOG2E = 1.4426950408889634

LORA_WINDOW = 384
RWKV_PROJ_COLS = 3584


def _cparams(sem):
    return pltpu.CompilerParams(dimension_semantics=sem, vmem_limit_bytes=VMEM_LIMIT)


def _bdot(a, b):
    return jnp.dot(a.astype(BF16), b.astype(BF16), preferred_element_type=F32)


def _bdot_nt(a, b):
    return lax.dot_general(a.astype(BF16), b.astype(BF16), (((1,), (1,)), ((), ())),
                           preferred_element_type=F32)


def _bdot_tn(a, b):
    return lax.dot_general(a.astype(BF16), b.astype(BF16), (((0,), (0,)), ((), ())),
                           preferred_element_type=F32)


def _split(x, terms):
    parts = []
    rem = x
    for _ in range(terms):
        hi = rem.astype(BF16)
        parts.append(hi)
        rem = rem - hi.astype(F32)
    return parts


def _dot_xe(x, e_bf16, terms):
    acc = None
    for part in _split(x, terms):
        d = jnp.dot(part, e_bf16, preferred_element_type=F32)
        acc = d if acc is None else acc + d
    return acc


def _dot_ex(e_bf16, x, terms):
    acc = None
    for part in _split(x, terms):
        d = jnp.dot(e_bf16, part, preferred_element_type=F32)
        acc = d if acc is None else acc + d
    return acc


def _dot3(a, b, nt=False):
    a_hi, a_lo = _split(a, 2)
    b_hi, b_lo = _split(b, 2)
    if nt:
        f = lambda u, v: lax.dot_general(u, v, (((1,), (1,)), ((), ())), preferred_element_type=F32)
    else:
        f = lambda u, v: jnp.dot(u, v, preferred_element_type=F32)
    return f(a_hi, b_hi) + (f(a_lo, b_hi) + f(a_hi, b_lo))


def _norm_mod(x, gain, shift, scale):
    y = x * lax.rsqrt(jnp.mean(x * x, axis=-1, keepdims=True) + RMS_EPS)
    return (y * gain) * (1.0 + scale) + shift


def _cast_block(w_ref, s_ref):
    if w_ref.dtype == BF16:
        return
    rows = w_ref.shape[0]
    step = CAST_ROWS if rows % CAST_ROWS == 0 else rows

    def body(r, carry):
        sl = pl.ds(pl.multiple_of(r * step, step), step)
        s_ref[sl, :] = w_ref[sl, :].astype(BF16)
        return carry

    lax.fori_loop(0, rows // step, body, 0)


def _mod_kernel(c_ref, w_ref, b_ref, o_ref):
    c = c_ref[...]
    o_ref[...] = _dot3(c * jax.nn.sigmoid(c), w_ref[...]) + b_ref[...]


def _adaln(c_pad, mod_w, mod_b, tn=512):
    m, d = c_pad.shape
    n = mod_w.shape[1]
    return pl.pallas_call(
        _mod_kernel,
        grid=(n // tn,),
        in_specs=[pl.BlockSpec((m, d), lambda j: (0, 0)),
                  pl.BlockSpec((d, tn), lambda j: (0, j)),
                  pl.BlockSpec((1, tn), lambda j: (0, j))],
        out_specs=pl.BlockSpec((m, tn), lambda j: (0, j)),
        out_shape=jax.ShapeDtypeStruct((m, n), F32),
        compiler_params=_cparams(("arbitrary",)),
        name="adaln",
    )(c_pad, mod_w, mod_b.reshape(1, n))


def _normmod_kernel(x_ref, g_ref, sh_ref, sc_ref, o_ref):
    o_ref[...] = _norm_mod(x_ref[...], g_ref[...], sh_ref[...], sc_ref[...]).astype(BF16)


def _normmod(x2, gain, mod3, seq, shift_idx, scale_idx, tm=512):
    m, d = x2.shape
    return pl.pallas_call(
        _normmod_kernel,
        grid=(m // tm,),
        in_specs=[pl.BlockSpec((tm, d), lambda i: (i, 0)),
                  pl.BlockSpec((1, d), lambda i: (0, 0)),
                  pl.BlockSpec((None, 1, d), lambda i: ((i * tm) // seq * 6 + shift_idx, 0, 0)),
                  pl.BlockSpec((None, 1, d), lambda i: ((i * tm) // seq * 6 + scale_idx, 0, 0))],
        out_specs=pl.BlockSpec((tm, d), lambda i: (i, 0)),
        out_shape=jax.ShapeDtypeStruct((m, d), BF16),
        compiler_params=_cparams(("arbitrary",)),
        name="normmod",
    )(x2, gain.reshape(1, d), mod3, mod3)


def _in_proj_kernel(h_ref, w_ref, o_ref, s_ref):
    @pl.when(pl.program_id(1) == 0)
    def _():
        s_ref[...] = w_ref[...].astype(BF16)

    o_ref[...] = jnp.dot(h_ref[...], s_ref[...], preferred_element_type=F32)


def _in_proj(h, w, n, tm=1024, tn=512):
    m, d = h.shape
    tm = min(tm, m)
    return pl.pallas_call(
        _in_proj_kernel,
        grid=(n // tn, m // tm),
        in_specs=[pl.BlockSpec((tm, d), lambda j, i: (i, 0)),
                  pl.BlockSpec((d, tn), lambda j, i: (0, j))],
        out_specs=pl.BlockSpec((tm, tn), lambda j, i: (i, j)),
        out_shape=jax.ShapeDtypeStruct((m, n), F32),
        scratch_shapes=[pltpu.VMEM((d, tn), BF16)],
        compiler_params=_cparams(("arbitrary", "arbitrary")),
        name="in_proj",
    )(h, w)


def _ffn_up_kernel(h_ref, wg_ref, wu_ref, o_ref, sg_ref, su_ref):
    @pl.when(pl.program_id(1) == 0)
    def _():
        _cast_block(wg_ref, sg_ref)
        _cast_block(wu_ref, su_ref)

    h = h_ref[...]
    gate = jnp.dot(h, sg_ref[...], preferred_element_type=F32)
    up = jnp.dot(h, su_ref[...], preferred_element_type=F32)
    o_ref[...] = (jax.nn.silu(gate) * up).astype(BF16)


def _ffn_up(h, wg, wu, tm=1024, tn=512):
    m, d = h.shape
    tm = min(tm, m)
    n = wg.shape[1]
    return pl.pallas_call(
        _ffn_up_kernel,
        grid=(n // tn, m // tm),
        in_specs=[pl.BlockSpec((tm, d), lambda j, i: (i, 0)),
                  pl.BlockSpec((d, tn), lambda j, i: (0, j)),
                  pl.BlockSpec((d, tn), lambda j, i: (0, j))],
        out_specs=pl.BlockSpec((tm, tn), lambda j, i: (i, j)),
        out_shape=jax.ShapeDtypeStruct((m, n), BF16),
        scratch_shapes=[pltpu.VMEM((d, tn), BF16), pltpu.VMEM((d, tn), BF16)],
        compiler_params=_cparams(("arbitrary", "arbitrary")),
        name="ffn_up",
    )(h, wg, wu)


def _mm_res_kernel(*refs, n_lhs):
    lhs = refs[:n_lhs]
    ws = refs[n_lhs:2 * n_lhs]
    res_ref, gate_ref, o_ref = refs[2 * n_lhs:2 * n_lhs + 3]
    scr = refs[2 * n_lhs + 3:]

    @pl.when(pl.program_id(1) == 0)
    def _():
        for w_ref, s_ref in zip(ws, scr):
            _cast_block(w_ref, s_ref)

    acc = None
    for l_ref, s_ref in zip(lhs, scr):
        d = jnp.dot(l_ref[...], s_ref[...], preferred_element_type=F32)
        acc = d if acc is None else acc + d
    o_ref[...] = res_ref[...] + gate_ref[...] * acc


def _mm_res(lhs_list, w, res, mod3, seq, gate_idx, tm=512, tn=512):
    m, n = res.shape
    tm = min(tm, m)
    k = lhs_list[0].shape[1]
    n_lhs = len(lhs_list)
    in_specs = [pl.BlockSpec((tm, k), lambda j, i: (i, 0)) for _ in lhs_list]
    in_specs += [pl.BlockSpec((k, tn), functools.partial(lambda j, i, li: (li, j), li=li))
                 for li in range(n_lhs)]
    in_specs += [pl.BlockSpec((tm, tn), lambda j, i: (i, j)),
                 pl.BlockSpec((None, 1, tn), lambda j, i: ((i * tm) // seq * 6 + gate_idx, 0, j))]
    return pl.pallas_call(
        functools.partial(_mm_res_kernel, n_lhs=n_lhs),
        grid=(n // tn, m // tm),
        in_specs=in_specs,
        out_specs=pl.BlockSpec((tm, tn), lambda j, i: (i, j)),
        out_shape=jax.ShapeDtypeStruct((m, n), F32),
        scratch_shapes=[pltpu.VMEM((k, tn), BF16) for _ in lhs_list],
        compiler_params=_cparams(("arbitrary", "arbitrary")),
        name="mm_res",
    )(*lhs_list, *([w] * n_lhs), res, mod3)


def _rwkv_kernel(*refs, chunk, width, vres):
    it = iter(refs)
    p_ref, mu_ref, w0_ref, w2_ref, a0_ref, a2_ref, g2_ref = (next(it) for _ in range(7))
    if vres:
        v0_ref, v2_ref, vf_ref = next(it), next(it), next(it)
    kk_ref, ka_ref, rk_ref, lnw_ref, lnb_ref = (next(it) for _ in range(5))
    o_ref = next(it)
    vf_out_ref = None if vres else next(it)
    carry_scr, state_scr = next(it), next(it)

    L, C = chunk, width
    P2 = 2 * L
    pairs = list(range(C // LANES))
    ci = pl.program_id(1)

    @pl.when(ci == 0)
    def _():
        carry_scr[...] = jnp.zeros_like(carry_scr)
        state_scr[...] = jnp.zeros_like(state_scr)

    def shifted(lo, hi):
        p = p_ref[:, lo:hi]
        rolled = pltpu.roll(p, 1, 0)
        row = lax.broadcasted_iota(jnp.int32, p.shape, 0)
        prev = jnp.where(row == 0, carry_scr[SUBLANES - 1:SUBLANES, lo:hi], rolled)
        return p + (prev - p) * mu_ref[:, lo:hi]

    zl = shifted(3 * C, 3 * C + LORA_WINDOW)
    wl = w0_ref[...] + _bdot(jnp.tanh(zl), w2_ref[...])
    logdecay = -jnp.exp(-jax.nn.softplus(-wl) - 0.5)
    a_all = jax.nn.sigmoid(a0_ref[...] + _bdot(zl, a2_ref[...]))
    g_all = _bdot(jax.nn.sigmoid(zl), g2_ref[...])
    if vres:
        vmix = jax.nn.sigmoid(v0_ref[...] + _bdot(zl, v2_ref[...]))

    tr = lax.broadcasted_iota(jnp.int32, (L, L), 0)
    tc = lax.broadcasted_iota(jnp.int32, (L, L), 1)
    tri = (tc <= tr).astype(BF16)
    cum = _dot_ex(tri, logdecay, 3)
    p_inc = jnp.exp(cum)
    p_exc = jnp.exp(cum - logdecay)
    p_inv = jnp.exp(-cum)

    rr = lax.broadcasted_iota(jnp.int32, (P2, P2), 0)
    cc = lax.broadcasted_iota(jnp.int32, (P2, P2), 1)
    strict = cc < rr
    incl = cc <= rr
    eye = (cc == rr).astype(F32)
    lane = lax.broadcasted_iota(jnp.int32, (L, LANES), 1)
    head0 = lane < RWKV_HEAD_DIM
    seg = ((rr // RWKV_HEAD_DIM) == (cc // RWKV_HEAD_DIM)).astype(BF16)

    def expand(a):
        return jnp.concatenate([jnp.where(head0, a, 0.0), jnp.where(head0, 0.0, a)], axis=0)

    sls = [slice(hp * LANES, (hp + 1) * LANES) for hp in pairs]
    def seg_sums(xs):
        tot = _dot_xe(jnp.concatenate(xs, axis=0), seg, 2)
        return [tot[i * L:(i + 1) * L] for i in range(len(xs))]

    r_l, k_l, v_l, kk0_l = [], [], [], []
    for hp in pairs:
        sl = sls[hp]
        r_l.append(shifted(hp * LANES, (hp + 1) * LANES))
        k_l.append(shifted(C + hp * LANES, C + (hp + 1) * LANES))
        v = shifted(2 * C + hp * LANES, 2 * C + (hp + 1) * LANES)
        if vres:
            v = v + (vf_ref[:, sl] - v) * vmix[:, sl]
        else:
            vf_out_ref[:, sl] = v
        v_l.append(v)
        kk0 = k_l[hp] * kk_ref[:, sl]
        kk0_l.append(kk0)
    ssq_l = seg_sums([kk0 * kk0 for kk0 in kk0_l])

    lhs_l, rhs_l, ax_l, vx_l, k2_l = [], [], [], [], []
    for hp in pairs:
        sl = sls[hp]
        a = a_all[:, sl]
        kk = kk0_l[hp] * lax.rsqrt(jnp.maximum(ssq_l[hp], 1e-24))
        k2 = k_l[hp] * (1.0 + (a - 1.0) * ka_ref[:, sl])
        k2_l.append(k2)
        ax = expand(-kk * p_exc[:, sl]).astype(BF16)
        rx = expand(r_l[hp] * p_inc[:, sl]).astype(BF16)
        bx = expand(kk * a * p_inv[:, sl]).astype(BF16)
        kx = expand(k2 * p_inv[:, sl]).astype(BF16)
        ax_l.append(ax)
        vx_l.append(expand(v_l[hp]).astype(BF16))
        lhs_l.append(jnp.concatenate([ax, rx], axis=0))
        rhs_l.append(jnp.concatenate([bx, kx], axis=0))

    gram_l = [_bdot_nt(lhs_l[hp], rhs_l[hp]) for hp in pairs]
    a_ak_l = [jnp.where(strict, g[:P2, P2:], 0.0).astype(BF16) for g in gram_l]
    a_r_l = [jnp.concatenate([jnp.where(incl, g[P2:, :P2], 0.0),
                              jnp.where(incl, g[P2:, P2:], 0.0)], axis=1).astype(BF16) for g in gram_l]

    n_l = [jnp.where(strict, g[:P2, :P2], 0.0) for g in gram_l]
    tinv_l = [eye + n for n in n_l]
    n_l = [_bdot(n, n) for n in n_l]
    span = 4
    while span < L:
        prod_l = [_bdot(jnp.concatenate([t, n], axis=0), n) for t, n in zip(tinv_l, n_l)]
        tinv_l = [t + p[:P2] for t, p in zip(tinv_l, prod_l)]
        n_l = [p[P2:] for p in prod_l]
        span *= 2
    tinv_l = [t + _bdot(t, n) for t, n in zip(tinv_l, n_l)]

    akv_l = [_bdot(a_ak_l[hp], vx_l[hp]) for hp in pairs]
    w_l = [_bdot(tinv_l[hp], jnp.concatenate([ax_l[hp], akv_l[hp].astype(BF16)], axis=1)) for hp in pairs]

    s0_l = [state_scr[hp] for hp in pairs]
    z_l = [_bdot_nt(jnp.concatenate([w_l[hp][:, :LANES].astype(BF16), lhs_l[hp][P2:]], axis=0), s0_l[hp])
           for hp in pairs]
    uv_l = [jnp.concatenate([(z_l[hp][:P2] + w_l[hp][:, LANES:]).astype(BF16), vx_l[hp]], axis=0)
            for hp in pairs]
    y2_l = [z_l[hp][P2:] + _bdot(a_r_l[hp], uv_l[hp]) for hp in pairs]
    for hp in pairs:
        s_new = s0_l[hp] + _bdot_tn(uv_l[hp], rhs_l[hp])
        state_scr[hp] = s_new * p_inc[L - 1:L, sls[hp]]

    y_l = [y2[:L] + y2[L:] for y2 in y2_l]
    mean_l = [t * (1.0 / RWKV_HEAD_DIM) for t in seg_sums(y_l)]
    dev_l = [y - mu for y, mu in zip(y_l, mean_l)]
    var_l = [t * (1.0 / RWKV_HEAD_DIM) for t in seg_sums([d * d for d in dev_l])]
    bon_l = seg_sums([r_l[hp] * k2_l[hp] * rk_ref[:, sls[hp]] for hp in pairs])
    for hp in pairs:
        sl = sls[hp]
        yn = dev_l[hp] * lax.rsqrt(var_l[hp] + GN_EPS) * lnw_ref[:, sl] + lnb_ref[:, sl]
        o_ref[:, sl] = ((yn + bon_l[hp] * v_l[hp]) * g_all[:, sl]).astype(o_ref.dtype)

    carry_scr[...] = p_ref[L - SUBLANES:L, :]


def _rwkv(proj, batch, seq, lp, v_first):
    vres = v_first is not None
    C = lp["w0"].shape[0]
    L = RWKV_CHUNK
    W = 3 * C + LORA_WINDOW
    row = lambda a: a.reshape(1, -1)
    full = lambda a: pl.BlockSpec(a.shape, lambda b, c: (0, 0))
    tok = pl.BlockSpec((L, C), lambda b, c: (b * (seq // L) + c, 0))
    args = [proj, lp["mu_pad"], row(lp["w0"]), lp["w2_pad"], row(lp["a0"]), lp["a2_pad"], lp["g2_pad"]]
    in_specs = [pl.BlockSpec((L, W), lambda b, c: (b * (seq // L) + c, 0))]
    in_specs += [full(a) for a in args[1:]]
    if vres:
        extra = [row(lp["v0"]), lp["v2_pad"]]
        args += extra + [v_first]
        in_specs += [full(a) for a in extra] + [tok]
    tail = [row(lp["k_k"]), row(lp["k_a"]), row(lp["r_k"]), row(lp["ln_w"]), row(lp["ln_b"])]
    args += tail
    in_specs += [full(a) for a in tail]
    y_sd = jax.ShapeDtypeStruct((batch * seq, C), BF16)
    v_sd = jax.ShapeDtypeStruct((batch * seq, C), F32)
    out = pl.pallas_call(
        functools.partial(_rwkv_kernel, chunk=L, width=C, vres=vres),
        grid=(batch, seq // L),
        in_specs=in_specs,
        out_specs=tok if vres else (tok, tok),
        out_shape=y_sd if vres else (y_sd, v_sd),
        scratch_shapes=[pltpu.VMEM((SUBLANES, W), F32),
                        pltpu.VMEM((C // LANES, LANES, LANES), F32)],
        compiler_params=_cparams(("arbitrary", "arbitrary")),
        name="rwkv7",
    )(*args)
    return (out, v_first) if vres else out


def _moba_kernel(q_ref, k_ref, v_ref, slope_ref, gain_ref, o_ref, *, n_blocks):
    BS = MOBA_BLOCK
    T = n_blocks * BS
    scale = MOBA_HEAD_DIM ** -0.5

    means = [jnp.mean(k_ref[j * BS:(j + 1) * BS, :], axis=0, keepdims=True) for j in range(n_blocks)]
    km = jnp.concatenate(means + [jnp.zeros((LANES - n_blocks, LANES), F32)], axis=0)
    gate_t = _dot3(km, q_ref[...], nt=True)[0:SUBLANES, :]
    blk = lax.broadcasted_iota(jnp.int32, (SUBLANES, T), 0)
    own_of = lax.broadcasted_iota(jnp.int32, (SUBLANES, T), 1) // BS
    past = blk < own_of
    gate_t = jnp.where(past, gate_t, -jnp.inf)
    beaten = jnp.zeros((SUBLANES, T), jnp.int32)
    for j in range(n_blocks):
        gj = gate_t[j:j + 1, :]
        ahead = (gj > gate_t) | ((gj == gate_t) & (blk > j))
        beaten = beaten + ahead.astype(jnp.int32)
    sel_t = ((beaten < MOBA_TOPK) & past).astype(F32)

    ind = (lax.broadcasted_iota(jnp.int32, (LANES, T), 1) // BS
           == lax.broadcasted_iota(jnp.int32, (LANES, T), 0)).astype(BF16)
    slope = slope_ref[:, 0:1] * LOG2E
    bias = slope * lax.broadcasted_iota(jnp.int32, (1, T), 1).astype(F32)
    causal = (lax.broadcasted_iota(jnp.int32, (BS, BS), 1)
              <= lax.broadcasted_iota(jnp.int32, (BS, BS), 0))
    zpad = jnp.zeros((LANES - SUBLANES, BS), F32)

    for own in range(n_blocks):
        lo, hi = own * BS, (own + 1) * BS
        q = (q_ref[lo:hi, :] * (scale * LOG2E)).astype(BF16)
        s_o = _bdot_nt(q, k_ref[lo:hi, :]) + bias[:, 0:BS]
        s_o = jnp.where(causal, s_o, -jnp.inf)
        m = jnp.max(s_o, axis=-1, keepdims=True)
        if own > 0:
            s_p = _bdot_nt(q, k_ref[0:lo, :]) + (bias[:, 0:lo] - slope * float(lo))
            if own > MOBA_TOPK:
                sel = jnp.concatenate([sel_t[:, lo:hi], zpad], axis=0).T.astype(BF16)
                picked = jnp.dot(sel, ind[:, 0:lo], preferred_element_type=F32) > 0.5
                s_p = jnp.where(picked, s_p, -jnp.inf)
            m = jnp.maximum(m, jnp.max(s_p, axis=-1, keepdims=True))
            e_p = jnp.exp2(s_p - m)
        e_o = jnp.exp2(s_o - m)
        denom = jnp.sum(e_o, axis=-1, keepdims=True)
        out = _bdot(e_o, v_ref[lo:hi, :])
        if own > 0:
            denom = denom + jnp.sum(e_p, axis=-1, keepdims=True)
            out = out + _bdot(e_p, v_ref[0:lo, :])
        out = out / denom
        out = out * lax.rsqrt(jnp.mean(out * out, axis=-1, keepdims=True) + RMS_EPS)
        o_ref[lo:hi, :] = (out * gain_ref[...]).astype(o_ref.dtype)


def _moba(proj, batch, seq, q_off, gain):
    width = gain.shape[0]
    heads = width // MOBA_HEAD_DIM
    nb = seq // MOBA_BLOCK
    assert nb <= SUBLANES and seq % MOBA_BLOCK == 0
    qo = q_off // LANES
    slopes = jnp.exp2(-8.0 * jnp.arange(1, heads + 1, dtype=F32) / heads)
    slopes = jnp.broadcast_to(slopes[:, None, None], (heads, 1, LANES))
    return pl.pallas_call(
        functools.partial(_moba_kernel, n_blocks=nb),
        grid=(batch, heads),
        in_specs=[pl.BlockSpec((seq, LANES), lambda b, h: (b, qo + h)),
                  pl.BlockSpec((seq, LANES), lambda b, h: (b, qo + heads + h)),
                  pl.BlockSpec((seq, LANES), lambda b, h: (b, qo + 2 * heads + h)),
                  pl.BlockSpec((None, 1, LANES), lambda b, h: (h, 0, 0)),
                  pl.BlockSpec((1, LANES), lambda b, h: (0, h))],
        out_specs=pl.BlockSpec((seq, LANES), lambda b, h: (b, h)),
        out_shape=jax.ShapeDtypeStruct((batch * seq, width), BF16),
        compiler_params=_cparams(("arbitrary", "arbitrary")),
        name="moba",
    )(proj, proj, proj, slopes, gain.reshape(1, width))


def _router_kernel(x_ref, g_ref, sh_ref, sc_ref, wr_ref, h_ref, route_ref, *, n_experts):
    h = _norm_mod(x_ref[...], g_ref[...], sh_ref[...], sc_ref[...])
    h_ref[...] = h
    logits = _dot3(h, wr_ref[...])
    col = lax.broadcasted_iota(jnp.int32, logits.shape, 1)
    neg = -jnp.inf
    logits = jnp.where(col < n_experts, logits, neg)
    m1 = jnp.max(logits, axis=-1, keepdims=True)
    i1 = jnp.min(jnp.where(logits == m1, col, LANES), axis=-1, keepdims=True)
    rest = jnp.where(col == i1, neg, logits)
    m2 = jnp.max(rest, axis=-1, keepdims=True)
    i2 = jnp.min(jnp.where(rest == m2, col, LANES), axis=-1, keepdims=True)
    e2 = jnp.exp(m2 - m1)
    wa = 1.0 / (1.0 + e2)
    wb = e2 / (1.0 + e2)
    route = jnp.where(col == 0, i1.astype(F32), 0.0)
    route = jnp.where(col == 1, i2.astype(F32), route)
    route = jnp.where(col == 2, wa, route)
    route = jnp.where(col == 3, wb, route)
    route_ref[...] = route


def _router(x2, gain, mod3, router_pad, seq, shift_idx, scale_idx, n_experts, tm=512):
    m, d = x2.shape
    sh = pl.BlockSpec((None, 1, d), lambda i: ((i * tm) // seq * 6 + shift_idx, 0, 0))
    sc = pl.BlockSpec((None, 1, d), lambda i: ((i * tm) // seq * 6 + scale_idx, 0, 0))
    return pl.pallas_call(
        functools.partial(_router_kernel, n_experts=n_experts),
        grid=(m // tm,),
        in_specs=[pl.BlockSpec((tm, d), lambda i: (i, 0)),
                  pl.BlockSpec((1, d), lambda i: (0, 0)),
                  sh, sc,
                  pl.BlockSpec((d, LANES), lambda i: (0, 0))],
        out_specs=(pl.BlockSpec((tm, d), lambda i: (i, 0)),
                   pl.BlockSpec((tm, LANES), lambda i: (i, 0))),
        out_shape=(jax.ShapeDtypeStruct((m, d), F32),
                   jax.ShapeDtypeStruct((m, LANES), F32)),
        compiler_params=_cparams(("arbitrary",)),
        name="router",
    )(x2, gain.reshape(1, d), mod3, mod3, router_pad)


def _to_bf16_kernel(x_ref, o_ref):
    o_ref[...] = x_ref[...].astype(BF16)


def _to_bf16(x, tm=1024):
    m, d = x.shape
    tm = min(tm, m)
    return pl.pallas_call(
        _to_bf16_kernel,
        grid=(m // tm,),
        in_specs=[pl.BlockSpec((tm, d), lambda i: (i, 0))],
        out_specs=pl.BlockSpec((tm, d), lambda i: (i, 0)),
        out_shape=jax.ShapeDtypeStruct((m, d), BF16),
        compiler_params=_cparams(("arbitrary",)),
        name="to_bf16",
    )(x)


def _sub_blocks(rows, o_ref, compute):
    for r0 in range(0, o_ref.shape[0], MOE_SUB):
        sl = slice(r0, r0 + MOE_SUB)

        @pl.when(rows > r0)
        def _(sl=sl):
            o_ref[sl, :] = compute(sl)

        @pl.when(rows <= r0)
        def _(sl=sl):
            o_ref[sl, :] = jnp.zeros((MOE_SUB, o_ref.shape[1]), o_ref.dtype)


def _moe_up_kernel(te_ref, tr_ref, tn_ref, x_ref, wg_ref, wu_ref, o_ref, sg_ref, su_ref):
    i = pl.program_id(1)

    @pl.when(tn_ref[i] > 0)
    def _():
        _cast_block(wg_ref, sg_ref)
        _cast_block(wu_ref, su_ref)

    def compute(sl):
        x = x_ref[sl, :]
        gate = jnp.dot(x, sg_ref[...], preferred_element_type=F32)
        up = jnp.dot(x, su_ref[...], preferred_element_type=F32)
        return (jax.nn.silu(gate) * up).astype(BF16)

    _sub_blocks(tr_ref[i], o_ref, compute)


def _moe_up(tiles, xs, wg, wu, tm, tf=512):
    p, d = xs.shape
    f = wg.shape[2]
    grid_spec = pltpu.PrefetchScalarGridSpec(
        num_scalar_prefetch=3,
        grid=(f // tf, p // tm),
        in_specs=[pl.BlockSpec((tm, d), lambda j, i, te, tv, tn: (i, 0)),
                  pl.BlockSpec((None, d, tf), lambda j, i, te, tv, tn: (te[i], 0, j)),
                  pl.BlockSpec((None, d, tf), lambda j, i, te, tv, tn: (te[i], 0, j))],
        out_specs=pl.BlockSpec((tm, tf), lambda j, i, te, tv, tn: (i, j)),
        scratch_shapes=[pltpu.VMEM((d, tf), BF16), pltpu.VMEM((d, tf), BF16)],
    )
    return pl.pallas_call(
        _moe_up_kernel,
        grid_spec=grid_spec,
        out_shape=jax.ShapeDtypeStruct((p, f), BF16),
        compiler_params=_cparams(("arbitrary", "arbitrary")),
        name="moe_up",
    )(*tiles, xs, wg, wu)


def _moe_down_kernel(te_ref, tr_ref, tn_ref, a_ref, wd_ref, o_ref, sd_ref):
    i = pl.program_id(1)

    @pl.when(tn_ref[i] > 0)
    def _():
        _cast_block(wd_ref, sd_ref)

    _sub_blocks(tr_ref[i], o_ref,
                lambda sl: jnp.dot(a_ref[sl, :], sd_ref[...], preferred_element_type=F32))


def _moe_down(tiles, act, wd, tm, tn=512):
    p, f = act.shape
    n = wd.shape[2]
    grid_spec = pltpu.PrefetchScalarGridSpec(
        num_scalar_prefetch=3,
        grid=(n // tn, p // tm),
        in_specs=[pl.BlockSpec((tm, f), lambda j, i, te, tv, tn_: (i, 0)),
                  pl.BlockSpec((None, f, tn), lambda j, i, te, tv, tn_: (te[i], 0, j))],
        out_specs=pl.BlockSpec((tm, tn), lambda j, i, te, tv, tn_: (i, j)),
        scratch_shapes=[pltpu.VMEM((f, tn), BF16)],
    )
    return pl.pallas_call(
        _moe_down_kernel,
        grid_spec=grid_spec,
        out_shape=jax.ShapeDtypeStruct((p, n), F32),
        compiler_params=_cparams(("arbitrary", "arbitrary")),
        name="moe_down",
    )(*tiles, act, wd)


def _moe_combine_kernel(x_ref, ya_ref, yb_ref, route_ref, gate_ref, *rest, final):
    route = route_ref[...]
    f = route[:, 2:3] * ya_ref[...] + route[:, 3:4] * yb_ref[...]
    x = x_ref[...] + gate_ref[...] * f
    if final:
        g_ref, o_ref = rest
        x = x * lax.rsqrt(jnp.mean(x * x, axis=-1, keepdims=True) + RMS_EPS) * g_ref[...]
    else:
        o_ref, = rest
    o_ref[...] = x


def _moe_combine(x2, yab, route, mod3, seq, gate_idx, final_gain, tm=256):
    m, d = x2.shape
    final = final_gain is not None
    in_specs = [pl.BlockSpec((tm, d), lambda i: (i, 0)),
                pl.BlockSpec((tm, d), lambda i: (i, 0)),
                pl.BlockSpec((tm, d), lambda i: (i + m // tm, 0)),
                pl.BlockSpec((tm, LANES), lambda i: (i, 0)),
                pl.BlockSpec((None, 1, d), lambda i: ((i * tm) // seq * 6 + gate_idx, 0, 0))]
    args = [x2, yab, yab, route, mod3]
    if final:
        in_specs.append(pl.BlockSpec((1, d), lambda i: (0, 0)))
        args.append(final_gain.reshape(1, d))
    return pl.pallas_call(
        functools.partial(_moe_combine_kernel, final=final),
        grid=(m // tm,),
        in_specs=in_specs,
        out_specs=pl.BlockSpec((tm, d), lambda i: (i, 0)),
        out_shape=jax.ShapeDtypeStruct((m, d), F32),
        compiler_params=_cparams(("arbitrary",)),
        name="moe_combine",
    )(*args)


def _moe(x2, gain, mod3, router, wg, wu, wd, seq, final_gain, tm=512):
    m, d = x2.shape
    n_experts = router.shape[1]
    router_pad = jnp.pad(router, ((0, 0), (0, LANES - n_experts)))
    h, route = _router(x2, gain, mod3, router_pad, seq, 3, 4, n_experts)

    slots = m * TOP_K
    n_tiles = slots // tm + n_experts
    flat_e = route[:, 0:TOP_K].astype(jnp.int32).reshape(slots)
    onehot = (flat_e[:, None] == jnp.arange(n_experts)[None, :]).astype(jnp.int32)
    rank = jnp.sum((jnp.cumsum(onehot, axis=0) - onehot) * onehot, axis=1)
    counts = jnp.sum(onehot, axis=0)
    padded = (counts + tm - 1) // tm * tm
    ends = jnp.cumsum(padded)
    dest = jnp.sum(onehot * (ends - padded)[None, :], axis=1) + rank
    sorted_tok = (jnp.arange(n_tiles * tm, dtype=jnp.int32) % m).at[dest].set(
        jnp.arange(slots, dtype=jnp.int32) // TOP_K)
    tile_start = jnp.arange(n_tiles, dtype=jnp.int32) * tm
    tile_expert = jnp.minimum(jnp.sum((tile_start[:, None] >= ends[None, :]).astype(jnp.int32), axis=1),
                              n_experts - 1)
    tile_rows = jnp.clip((ends - padded + counts)[tile_expert] - tile_start, 0, tm).astype(jnp.int32)
    tile_new = jnp.concatenate([jnp.ones((1,), jnp.int32),
                                (tile_expert[1:] != tile_expert[:-1]).astype(jnp.int32)])
    tiles = (tile_expert, tile_rows, tile_new)

    xs = _to_bf16(h.at[sorted_tok].get(mode="promise_in_bounds"))
    act = _moe_up(tiles, xs, wg, wu, tm)
    ys = _moe_down(tiles, act, wd, tm)
    dest_slot_major = dest.reshape(m, TOP_K).T.reshape(slots)
    yab = ys.at[dest_slot_major].get(mode="promise_in_bounds")
    return _moe_combine(x2, yab, route, mod3, seq, 5, final_gain)


def _final_norm_kernel(x_ref, g_ref, o_ref):
    x = x_ref[...]
    o_ref[...] = x * lax.rsqrt(jnp.mean(x * x, axis=-1, keepdims=True) + RMS_EPS) * g_ref[...]


def _final_norm(x2, gain, tm=512):
    m, d = x2.shape
    return pl.pallas_call(
        _final_norm_kernel,
        grid=(m // tm,),
        in_specs=[pl.BlockSpec((tm, d), lambda i: (i, 0)),
                  pl.BlockSpec((1, d), lambda i: (0, 0))],
        out_specs=pl.BlockSpec((tm, d), lambda i: (i, 0)),
        out_shape=jax.ShapeDtypeStruct((m, d), F32),
        compiler_params=_cparams(("arbitrary",)),
        name="final_norm",
    )(x2, gain.reshape(1, d))


def _prep_mixer(lp, C):
    w_in, mu = lp["w_in"], lp["shift_mu"]
    n_r = mu.shape[0]
    assert 3 * C < n_r <= 3 * C + LORA_WINDOW <= RWKV_PROJ_COLS
    out = dict(lp)
    out["w_in_m"] = w_in[:, n_r:]
    out["mu_pad"] = jnp.pad(mu, (0, 3 * C + LORA_WINDOW - n_r)).reshape(1, -1)
    off = 0
    for name in ("w2", "a2", "g2") + (("v2",) if "v2" in lp else ()):
        rank = lp[name].shape[0]
        out[name + "_pad"] = jnp.pad(lp[name], ((off, LORA_WINDOW - off - rank), (0, 0)))
        off += rank
    assert 3 * C + off == n_r
    return out


def _layer(x2, mod3, lp, batch, seq, v_first, final_gain):
    C = lp["w0"].shape[0]
    lp = _prep_mixer(lp, C)
    h = _normmod(x2, lp["norm_mix"], mod3, seq, 0, 1)
    y_r, v_first = _rwkv(_in_proj(h, lp["w_in"], RWKV_PROJ_COLS), batch, seq, lp, v_first)
    w_m = lp["w_in_m"]
    y_m = _moba(_in_proj(h, w_m, w_m.shape[1], tn=1024), batch, seq, 0, lp["moba_gain"])
    x2 = _mm_res([y_r, y_m], lp["w_out"], x2, mod3, seq, 2, tm=1024, tn=1024)
    if "ffn_gate" in lp:
        act = _ffn_up(_normmod(x2, lp["norm_ffn"], mod3, seq, 3, 4), lp["ffn_gate"], lp["ffn_up"])
        x2 = _mm_res([act], lp["ffn_down"], x2, mod3, seq, 5)
        if final_gain is not None:
            x2 = _final_norm(x2, final_gain)
    else:
        x2 = _moe(x2, lp["norm_ffn"], mod3, lp["router"], lp["exp_gate"], lp["exp_up"],
                  lp["exp_down"], seq, final_gain)
    return x2, v_first


def _forward(x, c, layers, norm_out):
    batch, seq, d = x.shape
    x2 = x.reshape(batch * seq, d)
    c_pad = jnp.pad(c, ((0, SUBLANES - batch), (0, 0)))
    v_first = None
    for li, lp in enumerate(layers):
        mod = _adaln(c_pad, lp["mod_w"], lp["mod_b"])[:batch]
        mod3 = mod.reshape(batch * 6, 1, d)
        last = li == len(layers) - 1
        x2, v_first = _layer(x2, mod3, lp, batch, seq, v_first, norm_out if last else None)
    return x2.reshape(batch, seq, d)


def kernel(x, c, l0_mod_w, l0_mod_b, l0_norm_mix, l0_w_in, l0_shift_mu, l0_w0, l0_w2, l0_a0, l0_a2, l0_g2, l0_k_k, l0_k_a, l0_r_k, l0_ln_w, l0_ln_b, l0_moba_gain, l0_w_out, l0_norm_ffn, l0_ffn_gate, l0_ffn_up, l0_ffn_down, l1_mod_w, l1_mod_b, l1_norm_mix, l1_w_in, l1_shift_mu, l1_w0, l1_w2, l1_a0, l1_a2, l1_g2, l1_v0, l1_v2, l1_k_k, l1_k_a, l1_r_k, l1_ln_w, l1_ln_b, l1_moba_gain, l1_w_out, l1_norm_ffn, l1_router, l1_exp_gate, l1_exp_up, l1_exp_down, norm_out):
    layers = (
        dict(mod_w=l0_mod_w, mod_b=l0_mod_b, norm_mix=l0_norm_mix, w_in=l0_w_in, shift_mu=l0_shift_mu,
             w0=l0_w0, w2=l0_w2, a0=l0_a0, a2=l0_a2, g2=l0_g2, k_k=l0_k_k, k_a=l0_k_a, r_k=l0_r_k,
             ln_w=l0_ln_w, ln_b=l0_ln_b, moba_gain=l0_moba_gain, w_out=l0_w_out, norm_ffn=l0_norm_ffn,
             ffn_gate=l0_ffn_gate, ffn_up=l0_ffn_up, ffn_down=l0_ffn_down),
        dict(mod_w=l1_mod_w, mod_b=l1_mod_b, norm_mix=l1_norm_mix, w_in=l1_w_in, shift_mu=l1_shift_mu,
             w0=l1_w0, w2=l1_w2, a0=l1_a0, a2=l1_a2, g2=l1_g2, v0=l1_v0, v2=l1_v2, k_k=l1_k_k,
             k_a=l1_k_a, r_k=l1_r_k, ln_w=l1_ln_w, ln_b=l1_ln_b, moba_gain=l1_moba_gain, w_out=l1_w_out,
             norm_ffn=l1_norm_ffn, router=l1_router, exp_gate=l1_exp_gate, exp_up=l1_exp_up,
             exp_down=l1_exp_down),
    )
    return _forward(x, c, layers, norm_out)
```

```python
import functools

import jax
import jax.numpy as jnp
from jax import lax
from jax.experimental import pallas as pl
from jax.experimental.pallas import tpu as pltpu

F32 = jnp.float32
BF16 = jnp.bfloat16

RMS_EPS = 1e-6
GN_EPS = 64e-5
RWKV_HEAD_DIM = 64
MOBA_HEAD_DIM = 128
MOBA_BLOCK = 256
MOBA_TOPK = 3
TOP_K = 2
LANES = 128
SUBLANES = 8
RWKV_CHUNK = 64
VMEM_LIMIT = 52 * 1024 * 1024
CAST_ROWS = 256
MOE_SUB = 512
LOG2E = 1.4426950408889634

LORA_WINDOW = 384
RWKV_PROJ_COLS = 3584


def _cparams(sem):
    return pltpu.CompilerParams(dimension_semantics=sem, vmem_limit_bytes=VMEM_LIMIT)


def _bdot(a, b):
    return jnp.dot(a.astype(BF16), b.astype(BF16), preferred_element_type=F32)


def _bdot_nt(a, b):
    return lax.dot_general(a.astype(BF16), b.astype(BF16), (((1,), (1,)), ((), ())),
                           preferred_element_type=F32)


def _bdot_tn(a, b):
    return lax.dot_general(a.astype(BF16), b.astype(BF16), (((0,), (0,)), ((), ())),
                           preferred_element_type=F32)


def _split(x, terms):
    parts = []
    rem = x
    for _ in range(terms):
        hi = rem.astype(BF16)
        parts.append(hi)
        rem = rem - hi.astype(F32)
    return parts


def _dot_xe(x, e_bf16, terms):
    acc = None
    for part in _split(x, terms):
        d = jnp.dot(part, e_bf16, preferred_element_type=F32)
        acc = d if acc is None else acc + d
    return acc


def _dot_ex(e_bf16, x, terms):
    acc = None
    for part in _split(x, terms):
        d = jnp.dot(e_bf16, part, preferred_element_type=F32)
        acc = d if acc is None else acc + d
    return acc


def _dot3(a, b, nt=False):
    a_hi, a_lo = _split(a, 2)
    b_hi, b_lo = _split(b, 2)
    if nt:
        f = lambda u, v: lax.dot_general(u, v, (((1,), (1,)), ((), ())), preferred_element_type=F32)
    else:
        f = lambda u, v: jnp.dot(u, v, preferred_element_type=F32)
    return f(a_hi, b_hi) + (f(a_lo, b_hi) + f(a_hi, b_lo))


def _norm_mod(x, gain, shift, scale):
    y = x * lax.rsqrt(jnp.mean(x * x, axis=-1, keepdims=True) + RMS_EPS)
    return (y * gain) * (1.0 + scale) + shift


def _cast_block(w_ref, s_ref):
    if w_ref.dtype == BF16:
        return
    rows = w_ref.shape[0]
    step = CAST_ROWS if rows % CAST_ROWS == 0 else rows

    def body(r, carry):
        sl = pl.ds(pl.multiple_of(r * step, step), step)
        s_ref[sl, :] = w_ref[sl, :].astype(BF16)
        return carry

    lax.fori_loop(0, rows // step, body, 0)


def _mod_kernel(c_ref, w_ref, b_ref, o_ref):
    c = c_ref[...]
    o_ref[...] = _dot3(c * jax.nn.sigmoid(c), w_ref[...]) + b_ref[...]


def _adaln(c_pad, mod_w, mod_b, tn=512):
    m, d = c_pad.shape
    n = mod_w.shape[1]
    return pl.pallas_call(
        _mod_kernel,
        grid=(n // tn,),
        in_specs=[pl.BlockSpec((m, d), lambda j: (0, 0)),
                  pl.BlockSpec((d, tn), lambda j: (0, j)),
                  pl.BlockSpec((1, tn), lambda j: (0, j))],
        out_specs=pl.BlockSpec((m, tn), lambda j: (0, j)),
        out_shape=jax.ShapeDtypeStruct((m, n), F32),
        compiler_params=_cparams(("arbitrary",)),
        name="adaln",
    )(c_pad, mod_w, mod_b.reshape(1, n))


def _normmod_kernel(x_ref, g_ref, sh_ref, sc_ref, o_ref):
    o_ref[...] = _norm_mod(x_ref[...], g_ref[...], sh_ref[...], sc_ref[...]).astype(BF16)


def _normmod(x2, gain, mod3, seq, shift_idx, scale_idx, tm=512):
    m, d = x2.shape
    return pl.pallas_call(
        _normmod_kernel,
        grid=(m // tm,),
        in_specs=[pl.BlockSpec((tm, d), lambda i: (i, 0)),
                  pl.BlockSpec((1, d), lambda i: (0, 0)),
                  pl.BlockSpec((None, 1, d), lambda i: ((i * tm) // seq * 6 + shift_idx, 0, 0)),
                  pl.BlockSpec((None, 1, d), lambda i: ((i * tm) // seq * 6 + scale_idx, 0, 0))],
        out_specs=pl.BlockSpec((tm, d), lambda i: (i, 0)),
        out_shape=jax.ShapeDtypeStruct((m, d), BF16),
        compiler_params=_cparams(("arbitrary",)),
        name="normmod",
    )(x2, gain.reshape(1, d), mod3, mod3)


def _in_proj_kernel(h_ref, w_ref, o_ref, s_ref):
    @pl.when(pl.program_id(1) == 0)
    def _():
        s_ref[...] = w_ref[...].astype(BF16)

    o_ref[...] = jnp.dot(h_ref[...], s_ref[...], preferred_element_type=F32)


def _in_proj(h, w, n, tm=1024, tn=512):
    m, d = h.shape
    tm = min(tm, m)
    return pl.pallas_call(
        _in_proj_kernel,
        grid=(n // tn, m // tm),
        in_specs=[pl.BlockSpec((tm, d), lambda j, i: (i, 0)),
                  pl.BlockSpec((d, tn), lambda j, i: (0, j))],
        out_specs=pl.BlockSpec((tm, tn), lambda j, i: (i, j)),
        out_shape=jax.ShapeDtypeStruct((m, n), F32),
        scratch_shapes=[pltpu.VMEM((d, tn), BF16)],
        compiler_params=_cparams(("arbitrary", "arbitrary")),
        name="in_proj",
    )(h, w)


def _in_proj_shift_kernel(h_ref, wa_ref, wb_ref, o_ref, s_ref, *, off):
    @pl.when(pl.program_id(1) == 0)
    def _():
        def body(r, carry):
            sl = pl.ds(pl.multiple_of(r * CAST_ROWS, CAST_ROWS), CAST_ROWS)
            a, b = wa_ref[sl, :], wb_ref[sl, :]
            s_ref[sl, :] = jnp.concatenate([a[:, off:], b[:, :off]], axis=1).astype(BF16)
            return carry

        lax.fori_loop(0, wa_ref.shape[0] // CAST_ROWS, body, 0)

    o_ref[...] = jnp.dot(h_ref[...], s_ref[...], preferred_element_type=F32)


def _in_proj_shifted(h, w, start, n, tm=2048, tn=256):
    m, d = h.shape
    tm = min(tm, m)
    off = start % LANES
    base = start - off
    assert 0 < off and base % tn == 0 and n % tn == 0 and d % CAST_ROWS == 0
    return pl.pallas_call(
        functools.partial(_in_proj_shift_kernel, off=off),
        grid=(n // tn, m // tm),
        in_specs=[pl.BlockSpec((tm, d), lambda j, i: (i, 0)),
                  pl.BlockSpec((d, tn), lambda j, i: (0, base // tn + j)),
                  pl.BlockSpec((d, LANES), lambda j, i: (0, (base + (j + 1) * tn) // LANES))],
        out_specs=pl.BlockSpec((tm, tn), lambda j, i: (i, j)),
        out_shape=jax.ShapeDtypeStruct((m, n), F32),
        scratch_shapes=[pltpu.VMEM((d, tn), BF16)],
        compiler_params=_cparams(("arbitrary", "arbitrary")),
        name="in_proj_shifted",
    )(h, w, w)


def _ffn_up_kernel(h_ref, wg_ref, wu_ref, o_ref, sg_ref, su_ref):
    @pl.when(pl.program_id(1) == 0)
    def _():
        _cast_block(wg_ref, sg_ref)
        _cast_block(wu_ref, su_ref)

    h = h_ref[...]
    gate = jnp.dot(h, sg_ref[...], preferred_element_type=F32)
    up = jnp.dot(h, su_ref[...], preferred_element_type=F32)
    o_ref[...] = (jax.nn.silu(gate) * up).astype(BF16)


def _ffn_up(h, wg, wu, tm=1024, tn=512):
    m, d = h.shape
    tm = min(tm, m)
    n = wg.shape[1]
    return pl.pallas_call(
        _ffn_up_kernel,
        grid=(n // tn, m // tm),
        in_specs=[pl.BlockSpec((tm, d), lambda j, i: (i, 0)),
                  pl.BlockSpec((d, tn), lambda j, i: (0, j)),
                  pl.BlockSpec((d, tn), lambda j, i: (0, j))],
        out_specs=pl.BlockSpec((tm, tn), lambda j, i: (i, j)),
        out_shape=jax.ShapeDtypeStruct((m, n), BF16),
        scratch_shapes=[pltpu.VMEM((d, tn), BF16), pltpu.VMEM((d, tn), BF16)],
        compiler_params=_cparams(("arbitrary", "arbitrary")),
        name="ffn_up",
    )(h, wg, wu)


def _mm_res_kernel(*refs, n_lhs):
    lhs = refs[:n_lhs]
    ws = refs[n_lhs:2 * n_lhs]
    res_ref, gate_ref, o_ref = refs[2 * n_lhs:2 * n_lhs + 3]
    scr = refs[2 * n_lhs + 3:]

    @pl.when(pl.program_id(1) == 0)
    def _():
        for w_ref, s_ref in zip(ws, scr):
            _cast_block(w_ref, s_ref)

    acc = None
    for l_ref, s_ref in zip(lhs, scr):
        d = jnp.dot(l_ref[...], s_ref[...], preferred_element_type=F32)
        acc = d if acc is None else acc + d
    o_ref[...] = res_ref[...] + gate_ref[...] * acc


def _mm_res(lhs_list, w, res, mod3, seq, gate_idx, tm=512, tn=512):
    m, n = res.shape
    tm = min(tm, m)
    k = lhs_list[0].shape[1]
    n_lhs = len(lhs_list)
    in_specs = [pl.BlockSpec((tm, k), lambda j, i: (i, 0)) for _ in lhs_list]
    in_specs += [pl.BlockSpec((k, tn), functools.partial(lambda j, i, li: (li, j), li=li))
                 for li in range(n_lhs)]
    in_specs += [pl.BlockSpec((tm, tn), lambda j, i: (i, j)),
                 pl.BlockSpec((None, 1, tn), lambda j, i: ((i * tm) // seq * 6 + gate_idx, 0, j))]
    return pl.pallas_call(
        functools.partial(_mm_res_kernel, n_lhs=n_lhs),
        grid=(n // tn, m // tm),
        in_specs=in_specs,
        out_specs=pl.BlockSpec((tm, tn), lambda j, i: (i, j)),
        out_shape=jax.ShapeDtypeStruct((m, n), F32),
        scratch_shapes=[pltpu.VMEM((k, tn), BF16) for _ in lhs_list],
        compiler_params=_cparams(("arbitrary", "arbitrary")),
        name="mm_res",
    )(*lhs_list, *([w] * n_lhs), res, mod3)


def _rwkv_kernel(*refs, chunk, width, vres):
    it = iter(refs)
    p_ref, mu_ref, w0_ref, w2_ref, a0_ref, a2_ref, g2_ref = (next(it) for _ in range(7))
    if vres:
        v0_ref, v2_ref, vf_ref = next(it), next(it), next(it)
    kk_ref, ka_ref, rk_ref, lnw_ref, lnb_ref = (next(it) for _ in range(5))
    o_ref = next(it)
    vf_out_ref = None if vres else next(it)
    carry_scr, state_scr = next(it), next(it)

    L, C = chunk, width
    P2 = 2 * L
    pairs = list(range(C // LANES))
    ci = pl.program_id(1)

    @pl.when(ci == 0)
    def _():
        carry_scr[...] = jnp.zeros_like(carry_scr)
        state_scr[...] = jnp.zeros_like(state_scr)

    def shifted(lo, hi):
        p = p_ref[:, lo:hi]
        rolled = pltpu.roll(p, 1, 0)
        row = lax.broadcasted_iota(jnp.int32, p.shape, 0)
        prev = jnp.where(row == 0, carry_scr[SUBLANES - 1:SUBLANES, lo:hi], rolled)
        return p + (prev - p) * mu_ref[:, lo:hi]

    zl = shifted(3 * C, 3 * C + LORA_WINDOW)
    wl = w0_ref[...] + _bdot(jnp.tanh(zl), w2_ref[...])
    logdecay = -jnp.exp(-jax.nn.softplus(-wl) - 0.5)
    a_all = jax.nn.sigmoid(a0_ref[...] + _bdot(zl, a2_ref[...]))
    g_all = _bdot(jax.nn.sigmoid(zl), g2_ref[...])
    if vres:
        vmix = jax.nn.sigmoid(v0_ref[...] + _bdot(zl, v2_ref[...]))

    tr = lax.broadcasted_iota(jnp.int32, (L, L), 0)
    tc = lax.broadcasted_iota(jnp.int32, (L, L), 1)
    tri = (tc <= tr).astype(BF16)
    cum = _dot_ex(tri, logdecay, 3)
    p_inc = jnp.exp(cum)
    p_exc = jnp.exp(cum - logdecay)
    p_inv = jnp.exp(-cum)

    rr = lax.broadcasted_iota(jnp.int32, (P2, P2), 0)
    cc = lax.broadcasted_iota(jnp.int32, (P2, P2), 1)
    strict = cc < rr
    incl = cc <= rr
    eye = (cc == rr).astype(F32)
    lane = lax.broadcasted_iota(jnp.int32, (L, LANES), 1)
    head0 = lane < RWKV_HEAD_DIM
    seg = ((rr // RWKV_HEAD_DIM) == (cc // RWKV_HEAD_DIM)).astype(BF16)

    def expand(a):
        return jnp.concatenate([jnp.where(head0, a, 0.0), jnp.where(head0, 0.0, a)], axis=0)

    sls = [slice(hp * LANES, (hp + 1) * LANES) for hp in pairs]
    def seg_sums(xs):
        tot = _dot_xe(jnp.concatenate(xs, axis=0), seg, 2)
        return [tot[i * L:(i + 1) * L] for i in range(len(xs))]

    r_l, k_l, v_l, kk0_l = [], [], [], []
    for hp in pairs:
        sl = sls[hp]
        r_l.append(shifted(hp * LANES, (hp + 1) * LANES))
        k_l.append(shifted(C + hp * LANES, C + (hp + 1) * LANES))
        v = shifted(2 * C + hp * LANES, 2 * C + (hp + 1) * LANES)
        if vres:
            v = v + (vf_ref[:, sl] - v) * vmix[:, sl]
        else:
            vf_out_ref[:, sl] = v
        v_l.append(v)
        kk0 = k_l[hp] * kk_ref[:, sl]
        kk0_l.append(kk0)
    ssq_l = seg_sums([kk0 * kk0 for kk0 in kk0_l])

    lhs_l, rhs_l, ax_l, vx_l, k2_l = [], [], [], [], []
    for hp in pairs:
        sl = sls[hp]
        a = a_all[:, sl]
        kk = kk0_l[hp] * lax.rsqrt(jnp.maximum(ssq_l[hp], 1e-24))
        k2 = k_l[hp] * (1.0 + (a - 1.0) * ka_ref[:, sl])
        k2_l.append(k2)
        ax = expand(-kk * p_exc[:, sl]).astype(BF16)
        rx = expand(r_l[hp] * p_inc[:, sl]).astype(BF16)
        bx = expand(kk * a * p_inv[:, sl]).astype(BF16)
        kx = expand(k2 * p_inv[:, sl]).astype(BF16)
        ax_l.append(ax)
        vx_l.append(expand(v_l[hp]).astype(BF16))
        lhs_l.append(jnp.concatenate([ax, rx], axis=0))
        rhs_l.append(jnp.concatenate([bx, kx], axis=0))

    gram_l = [_bdot_nt(lhs_l[hp], rhs_l[hp]) for hp in pairs]
    a_ak_l = [jnp.where(strict, g[:P2, P2:], 0.0).astype(BF16) for g in gram_l]
    a_r_l = [jnp.concatenate([jnp.where(incl, g[P2:, :P2], 0.0),
                              jnp.where(incl, g[P2:, P2:], 0.0)], axis=1).astype(BF16) for g in gram_l]

    n_l = [jnp.where(strict, g[:P2, :P2], 0.0) for g in gram_l]
    tinv_l = [eye + n for n in n_l]
    n_l = [_bdot(n, n) for n in n_l]
    span = 4
    while span < L:
        prod_l = [_bdot(jnp.concatenate([t, n], axis=0), n) for t, n in zip(tinv_l, n_l)]
        tinv_l = [t + p[:P2] for t, p in zip(tinv_l, prod_l)]
        n_l = [p[P2:] for p in prod_l]
        span *= 2
    tinv_l = [t + _bdot(t, n) for t, n in zip(tinv_l, n_l)]

    akv_l = [_bdot(a_ak_l[hp], vx_l[hp]) for hp in pairs]
    w_l = [_bdot(tinv_l[hp], jnp.concatenate([ax_l[hp], akv_l[hp].astype(BF16)], axis=1)) for hp in pairs]

    s0_l = [state_scr[hp] for hp in pairs]
    z_l = [_bdot_nt(jnp.concatenate([w_l[hp][:, :LANES].astype(BF16), lhs_l[hp][P2:]], axis=0), s0_l[hp])
           for hp in pairs]
    uv_l = [jnp.concatenate([(z_l[hp][:P2] + w_l[hp][:, LANES:]).astype(BF16), vx_l[hp]], axis=0)
            for hp in pairs]
    y2_l = [z_l[hp][P2:] + _bdot(a_r_l[hp], uv_l[hp]) for hp in pairs]
    for hp in pairs:
        s_new = s0_l[hp] + _bdot_tn(uv_l[hp], rhs_l[hp])
        state_scr[hp] = s_new * p_inc[L - 1:L, sls[hp]]

    y_l = [y2[:L] + y2[L:] for y2 in y2_l]
    mean_l = [t * (1.0 / RWKV_HEAD_DIM) for t in seg_sums(y_l)]
    dev_l = [y - mu for y, mu in zip(y_l, mean_l)]
    var_l = [t * (1.0 / RWKV_HEAD_DIM) for t in seg_sums([d * d for d in dev_l])]
    bon_l = seg_sums([r_l[hp] * k2_l[hp] * rk_ref[:, sls[hp]] for hp in pairs])
    for hp in pairs:
        sl = sls[hp]
        yn = dev_l[hp] * lax.rsqrt(var_l[hp] + GN_EPS) * lnw_ref[:, sl] + lnb_ref[:, sl]
        o_ref[:, sl] = ((yn + bon_l[hp] * v_l[hp]) * g_all[:, sl]).astype(o_ref.dtype)

    carry_scr[...] = p_ref[L - SUBLANES:L, :]


def _rwkv(proj, batch, seq, lp, v_first):
    vres = v_first is not None
    C = lp["w0"].shape[0]
    L = RWKV_CHUNK
    W = 3 * C + LORA_WINDOW
    row = lambda a: a.reshape(1, -1)
    full = lambda a: pl.BlockSpec(a.shape, lambda b, c: (0, 0))
    tok = pl.BlockSpec((L, C), lambda b, c: (b * (seq // L) + c, 0))
    args = [proj, lp["mu_pad"], row(lp["w0"]), lp["w2_pad"], row(lp["a0"]), lp["a2_pad"], lp["g2_pad"]]
    in_specs = [pl.BlockSpec((L, W), lambda b, c: (b * (seq // L) + c, 0))]
    in_specs += [full(a) for a in args[1:]]
    if vres:
        extra = [row(lp["v0"]), lp["v2_pad"]]
        args += extra + [v_first]
        in_specs += [full(a) for a in extra] + [tok]
    tail = [row(lp["k_k"]), row(lp["k_a"]), row(lp["r_k"]), row(lp["ln_w"]), row(lp["ln_b"])]
    args += tail
    in_specs += [full(a) for a in tail]
    y_sd = jax.ShapeDtypeStruct((batch * seq, C), BF16)
    v_sd = jax.ShapeDtypeStruct((batch * seq, C), F32)
    out = pl.pallas_call(
        functools.partial(_rwkv_kernel, chunk=L, width=C, vres=vres),
        grid=(batch, seq // L),
        in_specs=in_specs,
        out_specs=tok if vres else (tok, tok),
        out_shape=y_sd if vres else (y_sd, v_sd),
        scratch_shapes=[pltpu.VMEM((SUBLANES, W), F32),
                        pltpu.VMEM((C // LANES, LANES, LANES), F32)],
        compiler_params=_cparams(("arbitrary", "arbitrary")),
        name="rwkv7",
    )(*args)
    return (out, v_first) if vres else out


def _moba_kernel(q_ref, k_ref, v_ref, slope_ref, gain_ref, o_ref, *, n_blocks):
    BS = MOBA_BLOCK
    T = n_blocks * BS
    scale = MOBA_HEAD_DIM ** -0.5

    means = [jnp.mean(k_ref[j * BS:(j + 1) * BS, :], axis=0, keepdims=True) for j in range(n_blocks)]
    km = jnp.concatenate(means + [jnp.zeros((LANES - n_blocks, LANES), F32)], axis=0)
    gate_t = _dot3(km, q_ref[...], nt=True)[0:SUBLANES, :]
    blk = lax.broadcasted_iota(jnp.int32, (SUBLANES, T), 0)
    own_of = lax.broadcasted_iota(jnp.int32, (SUBLANES, T), 1) // BS
    past = blk < own_of
    gate_t = jnp.where(past, gate_t, -jnp.inf)
    beaten = jnp.zeros((SUBLANES, T), jnp.int32)
    for j in range(n_blocks):
        gj = gate_t[j:j + 1, :]
        ahead = (gj > gate_t) | ((gj == gate_t) & (blk > j))
        beaten = beaten + ahead.astype(jnp.int32)
    sel_t = ((beaten < MOBA_TOPK) & past).astype(F32)

    ind = (lax.broadcasted_iota(jnp.int32, (LANES, T), 1) // BS
           == lax.broadcasted_iota(jnp.int32, (LANES, T), 0)).astype(BF16)
    slope = slope_ref[:, 0:1] * LOG2E
    bias = slope * lax.broadcasted_iota(jnp.int32, (1, T), 1).astype(F32)
    causal = (lax.broadcasted_iota(jnp.int32, (BS, BS), 1)
              <= lax.broadcasted_iota(jnp.int32, (BS, BS), 0))
    zpad = jnp.zeros((LANES - SUBLANES, BS), F32)

    for own in range(n_blocks):
        lo, hi = own * BS, (own + 1) * BS
        q = (q_ref[lo:hi, :] * (scale * LOG2E)).astype(BF16)
        s_o = _bdot_nt(q, k_ref[lo:hi, :]) + bias[:, 0:BS]
        s_o = jnp.where(causal, s_o, -jnp.inf)
        m = jnp.max(s_o, axis=-1, keepdims=True)
        if own > 0:
            s_p = _bdot_nt(q, k_ref[0:lo, :]) + (bias[:, 0:lo] - slope * float(lo))
            if own > MOBA_TOPK:
                sel = jnp.concatenate([sel_t[:, lo:hi], zpad], axis=0).T.astype(BF16)
                picked = jnp.dot(sel, ind[:, 0:lo], preferred_element_type=F32) > 0.5
                s_p = jnp.where(picked, s_p, -jnp.inf)
            m = jnp.maximum(m, jnp.max(s_p, axis=-1, keepdims=True))
            e_p = jnp.exp2(s_p - m)
        e_o = jnp.exp2(s_o - m)
        denom = jnp.sum(e_o, axis=-1, keepdims=True)
        out = _bdot(e_o, v_ref[lo:hi, :])
        if own > 0:
            denom = denom + jnp.sum(e_p, axis=-1, keepdims=True)
            out = out + _bdot(e_p, v_ref[0:lo, :])
        out = out / denom
        out = out * lax.rsqrt(jnp.mean(out * out, axis=-1, keepdims=True) + RMS_EPS)
        o_ref[lo:hi, :] = (out * gain_ref[...]).astype(o_ref.dtype)


def _moba(proj, batch, seq, q_off, gain):
    width = gain.shape[0]
    heads = width // MOBA_HEAD_DIM
    nb = seq // MOBA_BLOCK
    assert nb <= SUBLANES and seq % MOBA_BLOCK == 0
    qo = q_off // LANES
    slopes = jnp.exp2(-8.0 * jnp.arange(1, heads + 1, dtype=F32) / heads)
    slopes = jnp.broadcast_to(slopes[:, None, None], (heads, 1, LANES))
    return pl.pallas_call(
        functools.partial(_moba_kernel, n_blocks=nb),
        grid=(batch, heads),
        in_specs=[pl.BlockSpec((seq, LANES), lambda b, h: (b, qo + h)),
                  pl.BlockSpec((seq, LANES), lambda b, h: (b, qo + heads + h)),
                  pl.BlockSpec((seq, LANES), lambda b, h: (b, qo + 2 * heads + h)),
                  pl.BlockSpec((None, 1, LANES), lambda b, h: (h, 0, 0)),
                  pl.BlockSpec((1, LANES), lambda b, h: (0, h))],
        out_specs=pl.BlockSpec((seq, LANES), lambda b, h: (b, h)),
        out_shape=jax.ShapeDtypeStruct((batch * seq, width), BF16),
        compiler_params=_cparams(("arbitrary", "arbitrary")),
        name="moba",
    )(proj, proj, proj, slopes, gain.reshape(1, width))


def _router_kernel(x_ref, g_ref, sh_ref, sc_ref, wr_ref, h_ref, route_ref, *, n_experts):
    h = _norm_mod(x_ref[...], g_ref[...], sh_ref[...], sc_ref[...])
    h_ref[...] = h
    logits = _dot3(h, wr_ref[...])
    col = lax.broadcasted_iota(jnp.int32, logits.shape, 1)
    neg = -jnp.inf
    logits = jnp.where(col < n_experts, logits, neg)
    m1 = jnp.max(logits, axis=-1, keepdims=True)
    i1 = jnp.min(jnp.where(logits == m1, col, LANES), axis=-1, keepdims=True)
    rest = jnp.where(col == i1, neg, logits)
    m2 = jnp.max(rest, axis=-1, keepdims=True)
    i2 = jnp.min(jnp.where(rest == m2, col, LANES), axis=-1, keepdims=True)
    e2 = jnp.exp(m2 - m1)
    wa = 1.0 / (1.0 + e2)
    wb = e2 / (1.0 + e2)
    route = jnp.where(col == 0, i1.astype(F32), 0.0)
    route = jnp.where(col == 1, i2.astype(F32), route)
    route = jnp.where(col == 2, wa, route)
    route = jnp.where(col == 3, wb, route)
    route_ref[...] = route


def _router(x2, gain, mod3, router_pad, seq, shift_idx, scale_idx, n_experts, tm=512):
    m, d = x2.shape
    sh = pl.BlockSpec((None, 1, d), lambda i: ((i * tm) // seq * 6 + shift_idx, 0, 0))
    sc = pl.BlockSpec((None, 1, d), lambda i: ((i * tm) // seq * 6 + scale_idx, 0, 0))
    return pl.pallas_call(
        functools.partial(_router_kernel, n_experts=n_experts),
        grid=(m // tm,),
        in_specs=[pl.BlockSpec((tm, d), lambda i: (i, 0)),
                  pl.BlockSpec((1, d), lambda i: (0, 0)),
                  sh, sc,
                  pl.BlockSpec((d, LANES), lambda i: (0, 0))],
        out_specs=(pl.BlockSpec((tm, d), lambda i: (i, 0)),
                   pl.BlockSpec((tm, LANES), lambda i: (i, 0))),
        out_shape=(jax.ShapeDtypeStruct((m, d), F32),
                   jax.ShapeDtypeStruct((m, LANES), F32)),
        compiler_params=_cparams(("arbitrary",)),
        name="router",
    )(x2, gain.reshape(1, d), mod3, mod3, router_pad)


def _to_bf16_kernel(x_ref, o_ref):
    o_ref[...] = x_ref[...].astype(BF16)


def _to_bf16(x, tm=1024):
    m, d = x.shape
    tm = min(tm, m)
    return pl.pallas_call(
        _to_bf16_kernel,
        grid=(m // tm,),
        in_specs=[pl.BlockSpec((tm, d), lambda i: (i, 0))],
        out_specs=pl.BlockSpec((tm, d), lambda i: (i, 0)),
        out_shape=jax.ShapeDtypeStruct((m, d), BF16),
        compiler_params=_cparams(("arbitrary",)),
        name="to_bf16",
    )(x)


def _sub_blocks(rows, o_ref, compute):
    for r0 in range(0, o_ref.shape[0], MOE_SUB):
        sl = slice(r0, r0 + MOE_SUB)

        @pl.when(rows > r0)
        def _(sl=sl):
            o_ref[sl, :] = compute(sl)

        @pl.when(rows <= r0)
        def _(sl=sl):
            o_ref[sl, :] = jnp.zeros((MOE_SUB, o_ref.shape[1]), o_ref.dtype)


def _moe_up_kernel(te_ref, tr_ref, tn_ref, x_ref, wg_ref, wu_ref, o_ref, sg_ref, su_ref):
    i = pl.program_id(1)

    @pl.when(tn_ref[i] > 0)
    def _():
        _cast_block(wg_ref, sg_ref)
        _cast_block(wu_ref, su_ref)

    def compute(sl):
        x = x_ref[sl, :]
        gate = jnp.dot(x, sg_ref[...], preferred_element_type=F32)
        up = jnp.dot(x, su_ref[...], preferred_element_type=F32)
        return (jax.nn.silu(gate) * up).astype(BF16)

    _sub_blocks(tr_ref[i], o_ref, compute)


def _moe_up(tiles, xs, wg, wu, tm, tf=512):
    p, d = xs.shape
    f = wg.shape[2]
    grid_spec = pltpu.PrefetchScalarGridSpec(
        num_scalar_prefetch=3,
        grid=(f // tf, p // tm),
        in_specs=[pl.BlockSpec((tm, d), lambda j, i, te, tv, tn: (i, 0)),
                  pl.BlockSpec((None, d, tf), lambda j, i, te, tv, tn: (te[i], 0, j)),
                  pl.BlockSpec((None, d, tf), lambda j, i, te, tv, tn: (te[i], 0, j))],
        out_specs=pl.BlockSpec((tm, tf), lambda j, i, te, tv, tn: (i, j)),
        scratch_shapes=[pltpu.VMEM((d, tf), BF16), pltpu.VMEM((d, tf), BF16)],
    )
    return pl.pallas_call(
        _moe_up_kernel,
        grid_spec=grid_spec,
        out_shape=jax.ShapeDtypeStruct((p, f), BF16),
        compiler_params=_cparams(("arbitrary", "arbitrary")),
        name="moe_up",
    )(*tiles, xs, wg, wu)


def _moe_down_kernel(te_ref, tr_ref, tn_ref, a_ref, wd_ref, o_ref, sd_ref):
    i = pl.program_id(1)

    @pl.when(tn_ref[i] > 0)
    def _():
        _cast_block(wd_ref, sd_ref)

    _sub_blocks(tr_ref[i], o_ref,
                lambda sl: jnp.dot(a_ref[sl, :], sd_ref[...], preferred_element_type=F32))


def _moe_down(tiles, act, wd, tm, tn=512):
    p, f = act.shape
    n = wd.shape[2]
    grid_spec = pltpu.PrefetchScalarGridSpec(
        num_scalar_prefetch=3,
        grid=(n // tn, p // tm),
        in_specs=[pl.BlockSpec((tm, f), lambda j, i, te, tv, tn_: (i, 0)),
                  pl.BlockSpec((None, f, tn), lambda j, i, te, tv, tn_: (te[i], 0, j))],
        out_specs=pl.BlockSpec((tm, tn), lambda j, i, te, tv, tn_: (i, j)),
        scratch_shapes=[pltpu.VMEM((f, tn), BF16)],
    )
    return pl.pallas_call(
        _moe_down_kernel,
        grid_spec=grid_spec,
        out_shape=jax.ShapeDtypeStruct((p, n), F32),
        compiler_params=_cparams(("arbitrary", "arbitrary")),
        name="moe_down",
    )(*tiles, act, wd)


def _moe_combine_kernel(x_ref, ya_ref, yb_ref, route_ref, gate_ref, *rest, final):
    route = route_ref[...]
    f = route[:, 2:3] * ya_ref[...] + route[:, 3:4] * yb_ref[...]
    x = x_ref[...] + gate_ref[...] * f
    if final:
        g_ref, o_ref = rest
        x = x * lax.rsqrt(jnp.mean(x * x, axis=-1, keepdims=True) + RMS_EPS) * g_ref[...]
    else:
        o_ref, = rest
    o_ref[...] = x


def _moe_combine(x2, yab, route, mod3, seq, gate_idx, final_gain, tm=256):
    m, d = x2.shape
    final = final_gain is not None
    in_specs = [pl.BlockSpec((tm, d), lambda i: (i, 0)),
                pl.BlockSpec((tm, d), lambda i: (i, 0)),
                pl.BlockSpec((tm, d), lambda i: (i + m // tm, 0)),
                pl.BlockSpec((tm, LANES), lambda i: (i, 0)),
                pl.BlockSpec((None, 1, d), lambda i: ((i * tm) // seq * 6 + gate_idx, 0, 0))]
    args = [x2, yab, yab, route, mod3]
    if final:
        in_specs.append(pl.BlockSpec((1, d), lambda i: (0, 0)))
        args.append(final_gain.reshape(1, d))
    return pl.pallas_call(
        functools.partial(_moe_combine_kernel, final=final),
        grid=(m // tm,),
        in_specs=in_specs,
        out_specs=pl.BlockSpec((tm, d), lambda i: (i, 0)),
        out_shape=jax.ShapeDtypeStruct((m, d), F32),
        compiler_params=_cparams(("arbitrary",)),
        name="moe_combine",
    )(*args)


def _moe(x2, gain, mod3, router, wg, wu, wd, seq, final_gain, tm=512):
    m, d = x2.shape
    n_experts = router.shape[1]
    router_pad = jnp.pad(router, ((0, 0), (0, LANES - n_experts)))
    h, route = _router(x2, gain, mod3, router_pad, seq, 3, 4, n_experts)

    slots = m * TOP_K
    n_tiles = slots // tm + n_experts
    flat_e = route[:, 0:TOP_K].astype(jnp.int32).reshape(slots)
    onehot = (flat_e[:, None] == jnp.arange(n_experts)[None, :]).astype(jnp.int32)
    rank = jnp.sum((jnp.cumsum(onehot, axis=0) - onehot) * onehot, axis=1)
    counts = jnp.sum(onehot, axis=0)
    padded = (counts + tm - 1) // tm * tm
    ends = jnp.cumsum(padded)
    dest = jnp.sum(onehot * (ends - padded)[None, :], axis=1) + rank
    sorted_tok = (jnp.arange(n_tiles * tm, dtype=jnp.int32) % m).at[dest].set(
        jnp.arange(slots, dtype=jnp.int32) // TOP_K)
    tile_start = jnp.arange(n_tiles, dtype=jnp.int32) * tm
    tile_expert = jnp.minimum(jnp.sum((tile_start[:, None] >= ends[None, :]).astype(jnp.int32), axis=1),
                              n_experts - 1)
    tile_rows = jnp.clip((ends - padded + counts)[tile_expert] - tile_start, 0, tm).astype(jnp.int32)
    tile_new = jnp.concatenate([jnp.ones((1,), jnp.int32),
                                (tile_expert[1:] != tile_expert[:-1]).astype(jnp.int32)])
    tiles = (tile_expert, tile_rows, tile_new)

    xs = _to_bf16(h.at[sorted_tok].get(mode="promise_in_bounds"))
    act = _moe_up(tiles, xs, wg, wu, tm)
    ys = _moe_down(tiles, act, wd, tm)
    dest_slot_major = dest.reshape(m, TOP_K).T.reshape(slots)
    yab = ys.at[dest_slot_major].get(mode="promise_in_bounds")
    return _moe_combine(x2, yab, route, mod3, seq, 5, final_gain)


def _final_norm_kernel(x_ref, g_ref, o_ref):
    x = x_ref[...]
    o_ref[...] = x * lax.rsqrt(jnp.mean(x * x, axis=-1, keepdims=True) + RMS_EPS) * g_ref[...]


def _final_norm(x2, gain, tm=512):
    m, d = x2.shape
    return pl.pallas_call(
        _final_norm_kernel,
        grid=(m // tm,),
        in_specs=[pl.BlockSpec((tm, d), lambda i: (i, 0)),
                  pl.BlockSpec((1, d), lambda i: (0, 0))],
        out_specs=pl.BlockSpec((tm, d), lambda i: (i, 0)),
        out_shape=jax.ShapeDtypeStruct((m, d), F32),
        compiler_params=_cparams(("arbitrary",)),
        name="final_norm",
    )(x2, gain.reshape(1, d))


def _prep_mixer(lp, C):
    w_in, mu = lp["w_in"], lp["shift_mu"]
    n_r = mu.shape[0]
    assert 3 * C < n_r <= 3 * C + LORA_WINDOW <= RWKV_PROJ_COLS
    out = dict(lp)
    out["mu_pad"] = jnp.pad(mu, (0, 3 * C + LORA_WINDOW - n_r)).reshape(1, -1)
    off = 0
    for name in ("w2", "a2", "g2") + (("v2",) if "v2" in lp else ()):
        rank = lp[name].shape[0]
        out[name + "_pad"] = jnp.pad(lp[name], ((off, LORA_WINDOW - off - rank), (0, 0)))
        off += rank
    assert 3 * C + off == n_r
    return out


def _layer(x2, mod3, lp, batch, seq, v_first, final_gain):
    C = lp["w0"].shape[0]
    lp = _prep_mixer(lp, C)
    h = _normmod(x2, lp["norm_mix"], mod3, seq, 0, 1)
    y_r, v_first = _rwkv(_in_proj(h, lp["w_in"], RWKV_PROJ_COLS), batch, seq, lp, v_first)
    n_r = lp["shift_mu"].shape[0]
    proj_m = _in_proj_shifted(h, lp["w_in"], n_r, lp["w_in"].shape[1] - n_r)
    y_m = _moba(proj_m, batch, seq, 0, lp["moba_gain"])
    x2 = _mm_res([y_r, y_m], lp["w_out"], x2, mod3, seq, 2, tm=1024, tn=1024)
    if "ffn_gate" in lp:
        act = _ffn_up(_normmod(x2, lp["norm_ffn"], mod3, seq, 3, 4), lp["ffn_gate"], lp["ffn_up"])
        x2 = _mm_res([act], lp["ffn_down"], x2, mod3, seq, 5)
        if final_gain is not None:
            x2 = _final_norm(x2, final_gain)
    else:
        x2 = _moe(x2, lp["norm_ffn"], mod3, lp["router"], lp["exp_gate"], lp["exp_up"],
                  lp["exp_down"], seq, final_gain)
    return x2, v_first


def _forward(x, c, layers, norm_out):
    batch, seq, d = x.shape
    x2 = x.reshape(batch * seq, d)
    c_pad = jnp.pad(c, ((0, SUBLANES - batch), (0, 0)))
    v_first = None
    for li, lp in enumerate(layers):
        mod = _adaln(c_pad, lp["mod_w"], lp["mod_b"])[:batch]
        mod3 = mod.reshape(batch * 6, 1, d)
        last = li == len(layers) - 1
        x2, v_first = _layer(x2, mod3, lp, batch, seq, v_first, norm_out if last else None)
    return x2.reshape(batch, seq, d)


def kernel(x, c, l0_mod_w, l0_mod_b, l0_norm_mix, l0_w_in, l0_shift_mu, l0_w0, l0_w2, l0_a0, l0_a2, l0_g2, l0_k_k, l0_k_a, l0_r_k, l0_ln_w, l0_ln_b, l0_moba_gain, l0_w_out, l0_norm_ffn, l0_ffn_gate, l0_ffn_up, l0_ffn_down, l1_mod_w, l1_mod_b, l1_norm_mix, l1_w_in, l1_shift_mu, l1_w0, l1_w2, l1_a0, l1_a2, l1_g2, l1_v0, l1_v2, l1_k_k, l1_k_a, l1_r_k, l1_ln_w, l1_ln_b, l1_moba_gain, l1_w_out, l1_norm_ffn, l1_router, l1_exp_gate, l1_exp_up, l1_exp_down, norm_out):
    layers = (
        dict(mod_w=l0_mod_w, mod_b=l0_mod_b, norm_mix=l0_norm_mix, w_in=l0_w_in, shift_mu=l0_shift_mu,
             w0=l0_w0, w2=l0_w2, a0=l0_a0, a2=l0_a2, g2=l0_g2, k_k=l0_k_k, k_a=l0_k_a, r_k=l0_r_k,
             ln_w=l0_ln_w, ln_b=l0_ln_b, moba_gain=l0_moba_gain, w_out=l0_w_out, norm_ffn=l0_norm_ffn,
             ffn_gate=l0_ffn_gate, ffn_up=l0_ffn_up, ffn_down=l0_ffn_down),
        dict(mod_w=l1_mod_w, mod_b=l1_mod_b, norm_mix=l1_norm_mix, w_in=l1_w_in, shift_mu=l1_shift_mu,
             w0=l1_w0, w2=l1_w2, a0=l1_a0, a2=l1_a2, g2=l1_g2, v0=l1_v0, v2=l1_v2, k_k=l1_k_k,
             k_a=l1_k_a, r_k=l1_r_k, ln_w=l1_ln_w, ln_b=l1_ln_b, moba_gain=l1_moba_gain, w_out=l1_w_out,
             norm_ffn=l1_norm_ffn, router=l1_router, exp_gate=l1_exp_gate, exp_up=l1_exp_up,
             exp_down=l1_exp_down),
    )
    return _forward(x, c, layers, norm_out)
```

```python
import functools

import jax
import jax.numpy as jnp
from jax import lax
from jax.experimental import pallas as pl
from jax.experimental.pallas import tpu as pltpu

F32 = jnp.float32
BF16 = jnp.bfloat16

RMS_EPS = 1e-6
GN_EPS = 64e-5
RWKV_HEAD_DIM = 64
MOBA_HEAD_DIM = 128
MOBA_BLOCK = 256
MOBA_TOPK = 3
TOP_K = 2
LANES = 128
SUBLANES = 8
RWKV_CHUNK = 64
VMEM_LIMIT = 52 * 1024 * 1024
CAST_ROWS = 256
LOG2E = 1.4426950408889634

LORA_WINDOW = 384
RWKV_PROJ_COLS = 3584


def _cparams(sem):
    return pltpu.CompilerParams(dimension_semantics=sem, vmem_limit_bytes=VMEM_LIMIT)


def _bdot(a, b):
    return jnp.dot(a.astype(BF16), b.astype(BF16), preferred_element_type=F32)


def _bdot_nt(a, b):
    return lax.dot_general(a.astype(BF16), b.astype(BF16), (((1,), (1,)), ((), ())),
                           preferred_element_type=F32)


def _bdot_tn(a, b):
    return lax.dot_general(a.astype(BF16), b.astype(BF16), (((0,), (0,)), ((), ())),
                           preferred_element_type=F32)


def _split(x, terms):
    parts = []
    rem = x
    for _ in range(terms):
        hi = rem.astype(BF16)
        parts.append(hi)
        rem = rem - hi.astype(F32)
    return parts


def _dot_xe(x, e_bf16, terms):
    acc = None
    for part in _split(x, terms):
        d = jnp.dot(part, e_bf16, preferred_element_type=F32)
        acc = d if acc is None else acc + d
    return acc


def _dot_ex(e_bf16, x, terms):
    acc = None
    for part in _split(x, terms):
        d = jnp.dot(e_bf16, part, preferred_element_type=F32)
        acc = d if acc is None else acc + d
    return acc


def _dot3(a, b, nt=False):
    a_hi, a_lo = _split(a, 2)
    b_hi, b_lo = _split(b, 2)
    if nt:
        f = lambda u, v: lax.dot_general(u, v, (((1,), (1,)), ((), ())), preferred_element_type=F32)
    else:
        f = lambda u, v: jnp.dot(u, v, preferred_element_type=F32)
    return f(a_hi, b_hi) + (f(a_lo, b_hi) + f(a_hi, b_lo))


def _norm_mod(x, gain, shift, scale):
    y = x * lax.rsqrt(jnp.mean(x * x, axis=-1, keepdims=True) + RMS_EPS)
    return (y * gain) * (1.0 + scale) + shift


def _cast_block(w_ref, s_ref):
    if w_ref.dtype == BF16:
        return
    rows = w_ref.shape[0]
    step = CAST_ROWS if rows % CAST_ROWS == 0 else rows

    def body(r, carry):
        sl = pl.ds(pl.multiple_of(r * step, step), step)
        s_ref[sl, :] = w_ref[sl, :].astype(BF16)
        return carry

    lax.fori_loop(0, rows // step, body, 0)


def _mod_kernel(c_ref, w_ref, b_ref, o_ref):
    c = c_ref[...]
    o_ref[...] = _dot3(c * jax.nn.sigmoid(c), w_ref[...]) + b_ref[...]


def _adaln(c_pad, mod_w, mod_b, tn=512):
    m, d = c_pad.shape
    n = mod_w.shape[1]
    return pl.pallas_call(
        _mod_kernel,
        grid=(n // tn,),
        in_specs=[pl.BlockSpec((m, d), lambda j: (0, 0)),
                  pl.BlockSpec((d, tn), lambda j: (0, j)),
                  pl.BlockSpec((1, tn), lambda j: (0, j))],
        out_specs=pl.BlockSpec((m, tn), lambda j: (0, j)),
        out_shape=jax.ShapeDtypeStruct((m, n), F32),
        compiler_params=_cparams(("arbitrary",)),
        name="adaln",
    )(c_pad, mod_w, mod_b.reshape(1, n))


def _normmod_kernel(x_ref, g_ref, sh_ref, sc_ref, o_ref):
    o_ref[...] = _norm_mod(x_ref[...], g_ref[...], sh_ref[...], sc_ref[...]).astype(BF16)


def _normmod(x2, gain, mod3, seq, shift_idx, scale_idx, tm=512):
    m, d = x2.shape
    return pl.pallas_call(
        _normmod_kernel,
        grid=(m // tm,),
        in_specs=[pl.BlockSpec((tm, d), lambda i: (i, 0)),
                  pl.BlockSpec((1, d), lambda i: (0, 0)),
                  pl.BlockSpec((None, 1, d), lambda i: ((i * tm) // seq * 6 + shift_idx, 0, 0)),
                  pl.BlockSpec((None, 1, d), lambda i: ((i * tm) // seq * 6 + scale_idx, 0, 0))],
        out_specs=pl.BlockSpec((tm, d), lambda i: (i, 0)),
        out_shape=jax.ShapeDtypeStruct((m, d), BF16),
        compiler_params=_cparams(("arbitrary",)),
        name="normmod",
    )(x2, gain.reshape(1, d), mod3, mod3)


def _in_proj_kernel(h_ref, w_ref, o_ref):
    o_ref[...] = jnp.dot(h_ref[...], w_ref[...], preferred_element_type=F32)


def _in_proj(h, w, tm=1024):
    m, d = h.shape
    n = w.shape[1]
    tm = min(tm, m)
    tn = n // 2
    assert tn % (2 * LANES) == 0
    return pl.pallas_call(
        _in_proj_kernel,
        grid=(n // tn, m // tm),
        in_specs=[pl.BlockSpec((tm, d), lambda j, i: (i, 0)),
                  pl.BlockSpec((d, tn), lambda j, i: (0, j))],
        out_specs=pl.BlockSpec((tm, tn), lambda j, i: (i, j)),
        out_shape=jax.ShapeDtypeStruct((m, n), F32),
        compiler_params=_cparams(("arbitrary", "arbitrary")),
        name="in_proj",
    )(h, w)


def _ffn_up_kernel(h_ref, wg_ref, wu_ref, o_ref, sg_ref, su_ref):
    @pl.when(pl.program_id(1) == 0)
    def _():
        _cast_block(wg_ref, sg_ref)
        _cast_block(wu_ref, su_ref)

    h = h_ref[...]
    gate = jnp.dot(h, sg_ref[...], preferred_element_type=F32)
    up = jnp.dot(h, su_ref[...], preferred_element_type=F32)
    o_ref[...] = (jax.nn.silu(gate) * up).astype(BF16)


def _ffn_up(h, wg, wu, tm=1024, tn=512):
    m, d = h.shape
    tm = min(tm, m)
    n = wg.shape[1]
    return pl.pallas_call(
        _ffn_up_kernel,
        grid=(n // tn, m // tm),
        in_specs=[pl.BlockSpec((tm, d), lambda j, i: (i, 0)),
                  pl.BlockSpec((d, tn), lambda j, i: (0, j)),
                  pl.BlockSpec((d, tn), lambda j, i: (0, j))],
        out_specs=pl.BlockSpec((tm, tn), lambda j, i: (i, j)),
        out_shape=jax.ShapeDtypeStruct((m, n), BF16),
        scratch_shapes=[pltpu.VMEM((d, tn), BF16), pltpu.VMEM((d, tn), BF16)],
        compiler_params=_cparams(("arbitrary", "arbitrary")),
        name="ffn_up",
    )(h, wg, wu)


def _mm_res_kernel(*refs, n_lhs):
    lhs = refs[:n_lhs]
    ws = refs[n_lhs:2 * n_lhs]
    res_ref, gate_ref, o_ref = refs[2 * n_lhs:2 * n_lhs + 3]
    scr = refs[2 * n_lhs + 3:]

    @pl.when(pl.program_id(1) == 0)
    def _():
        for w_ref, s_ref in zip(ws, scr):
            _cast_block(w_ref, s_ref)

    acc = None
    for l_ref, s_ref in zip(lhs, scr):
        d = jnp.dot(l_ref[...], s_ref[...], preferred_element_type=F32)
        acc = d if acc is None else acc + d
    o_ref[...] = res_ref[...] + gate_ref[...] * acc


def _mm_res(lhs_list, w, res, mod3, seq, gate_idx, tm=512, tn=512):
    m, n = res.shape
    tm = min(tm, m)
    k = lhs_list[0].shape[1]
    n_lhs = len(lhs_list)
    in_specs = [pl.BlockSpec((tm, k), lambda j, i: (i, 0)) for _ in lhs_list]
    in_specs += [pl.BlockSpec((k, tn), functools.partial(lambda j, i, li: (li, j), li=li))
                 for li in range(n_lhs)]
    in_specs += [pl.BlockSpec((tm, tn), lambda j, i: (i, j)),
                 pl.BlockSpec((None, 1, tn), lambda j, i: ((i * tm) // seq * 6 + gate_idx, 0, j))]
    return pl.pallas_call(
        functools.partial(_mm_res_kernel, n_lhs=n_lhs),
        grid=(n // tn, m // tm),
        in_specs=in_specs,
        out_specs=pl.BlockSpec((tm, tn), lambda j, i: (i, j)),
        out_shape=jax.ShapeDtypeStruct((m, n), F32),
        scratch_shapes=[pltpu.VMEM((k, tn), BF16) for _ in lhs_list],
        compiler_params=_cparams(("arbitrary", "arbitrary")),
        name="mm_res",
    )(*lhs_list, *([w] * n_lhs), res, mod3)


def _rwkv_kernel(*refs, chunk, width, vres):
    it = iter(refs)
    p_ref, mu_ref, w0_ref, w2_ref, a0_ref, a2_ref, g2_ref = (next(it) for _ in range(7))
    if vres:
        v0_ref, v2_ref, vf_ref = next(it), next(it), next(it)
    kk_ref, ka_ref, rk_ref, lnw_ref, lnb_ref = (next(it) for _ in range(5))
    o_ref = next(it)
    vf_out_ref = None if vres else next(it)
    carry_scr, state_scr = next(it), next(it)

    L, C = chunk, width
    P2 = 2 * L
    pairs = list(range(C // LANES))
    ci = pl.program_id(1)

    @pl.when(ci == 0)
    def _():
        carry_scr[...] = jnp.zeros_like(carry_scr)
        state_scr[...] = jnp.zeros_like(state_scr)

    def shifted(lo, hi):
        p = p_ref[:, lo:hi]
        rolled = pltpu.roll(p, 1, 0)
        row = lax.broadcasted_iota(jnp.int32, p.shape, 0)
        prev = jnp.where(row == 0, carry_scr[SUBLANES - 1:SUBLANES, lo:hi], rolled)
        return p + (prev - p) * mu_ref[:, lo:hi]

    zl = shifted(3 * C, 3 * C + LORA_WINDOW)
    wl = w0_ref[...] + _bdot(jnp.tanh(zl), w2_ref[...])
    logdecay = -jnp.exp(-jax.nn.softplus(-wl) - 0.5)
    a_all = jax.nn.sigmoid(a0_ref[...] + _bdot(zl, a2_ref[...]))
    g_all = _bdot(jax.nn.sigmoid(zl), g2_ref[...])
    if vres:
        vmix = jax.nn.sigmoid(v0_ref[...] + _bdot(zl, v2_ref[...]))

    tr = lax.broadcasted_iota(jnp.int32, (L, L), 0)
    tc = lax.broadcasted_iota(jnp.int32, (L, L), 1)
    tri = (tc <= tr).astype(BF16)
    cum = _dot_ex(tri, logdecay, 3)
    p_inc = jnp.exp(cum)
    p_exc = jnp.exp(cum - logdecay)
    p_inv = jnp.exp(-cum)

    rr = lax.broadcasted_iota(jnp.int32, (P2, P2), 0)
    cc = lax.broadcasted_iota(jnp.int32, (P2, P2), 1)
    strict = cc < rr
    incl = cc <= rr
    eye = (cc == rr).astype(F32)
    lane = lax.broadcasted_iota(jnp.int32, (L, LANES), 1)
    head0 = lane < RWKV_HEAD_DIM
    seg = ((rr // RWKV_HEAD_DIM) == (cc // RWKV_HEAD_DIM)).astype(BF16)

    def expand(a):
        return jnp.concatenate([jnp.where(head0, a, 0.0), jnp.where(head0, 0.0, a)], axis=0)

    sls = [slice(hp * LANES, (hp + 1) * LANES) for hp in pairs]
    def seg_sums(xs):
        tot = _dot_xe(jnp.concatenate(xs, axis=0), seg, 2)
        return [tot[i * L:(i + 1) * L] for i in range(len(xs))]

    r_l, k_l, v_l, kk0_l = [], [], [], []
    for hp in pairs:
        sl = sls[hp]
        r_l.append(shifted(hp * LANES, (hp + 1) * LANES))
        k_l.append(shifted(C + hp * LANES, C + (hp + 1) * LANES))
        v = shifted(2 * C + hp * LANES, 2 * C + (hp + 1) * LANES)
        if vres:
            v = v + (vf_ref[:, sl] - v) * vmix[:, sl]
        else:
            vf_out_ref[:, sl] = v
        v_l.append(v)
        kk0 = k_l[hp] * kk_ref[:, sl]
        kk0_l.append(kk0)
    ssq_l = seg_sums([kk0 * kk0 for kk0 in kk0_l])

    lhs_l, rhs_l, ax_l, vx_l, k2_l = [], [], [], [], []
    for hp in pairs:
        sl = sls[hp]
        a = a_all[:, sl]
        kk = kk0_l[hp] * lax.rsqrt(jnp.maximum(ssq_l[hp], 1e-24))
        k2 = k_l[hp] * (1.0 + (a - 1.0) * ka_ref[:, sl])
        k2_l.append(k2)
        ax = expand(-kk * p_exc[:, sl]).astype(BF16)
        rx = expand(r_l[hp] * p_inc[:, sl]).astype(BF16)
        bx = expand(kk * a * p_inv[:, sl]).astype(BF16)
        kx = expand(k2 * p_inv[:, sl]).astype(BF16)
        ax_l.append(ax)
        vx_l.append(expand(v_l[hp]).astype(BF16))
        lhs_l.append(jnp.concatenate([ax, rx], axis=0))
        rhs_l.append(jnp.concatenate([bx, kx], axis=0))

    gram_l = [_bdot_nt(lhs_l[hp], rhs_l[hp]) for hp in pairs]
    a_ak_l = [jnp.where(strict, g[:P2, P2:], 0.0).astype(BF16) for g in gram_l]
    a_r_l = [jnp.concatenate([jnp.where(incl, g[P2:, :P2], 0.0),
                              jnp.where(incl, g[P2:, P2:], 0.0)], axis=1).astype(BF16) for g in gram_l]

    n_l = [jnp.where(strict, g[:P2, :P2], 0.0) for g in gram_l]
    tinv_l = [eye + n for n in n_l]
    n_l = [_bdot(n, n) for n in n_l]
    span = 4
    while span < L:
        prod_l = [_bdot(jnp.concatenate([t, n], axis=0), n) for t, n in zip(tinv_l, n_l)]
        tinv_l = [t + p[:P2] for t, p in zip(tinv_l, prod_l)]
        n_l = [p[P2:] for p in prod_l]
        span *= 2
    tinv_l = [t + _bdot(t, n) for t, n in zip(tinv_l, n_l)]

    akv_l = [_bdot(a_ak_l[hp], vx_l[hp]) for hp in pairs]
    w_l = [_bdot(tinv_l[hp], jnp.concatenate([ax_l[hp], akv_l[hp].astype(BF16)], axis=1)) for hp in pairs]

    s0_l = [state_scr[hp] for hp in pairs]
    z_l = [_bdot_nt(jnp.concatenate([w_l[hp][:, :LANES].astype(BF16), lhs_l[hp][P2:]], axis=0), s0_l[hp])
           for hp in pairs]
    uv_l = [jnp.concatenate([(z_l[hp][:P2] + w_l[hp][:, LANES:]).astype(BF16), vx_l[hp]], axis=0)
            for hp in pairs]
    y2_l = [z_l[hp][P2:] + _bdot(a_r_l[hp], uv_l[hp]) for hp in pairs]
    for hp in pairs:
        s_new = s0_l[hp] + _bdot_tn(uv_l[hp], rhs_l[hp])
        state_scr[hp] = s_new * p_inc[L - 1:L, sls[hp]]

    y_l = [y2[:L] + y2[L:] for y2 in y2_l]
    mean_l = [t * (1.0 / RWKV_HEAD_DIM) for t in seg_sums(y_l)]
    dev_l = [y - mu for y, mu in zip(y_l, mean_l)]
    var_l = [t * (1.0 / RWKV_HEAD_DIM) for t in seg_sums([d * d for d in dev_l])]
    bon_l = seg_sums([r_l[hp] * k2_l[hp] * rk_ref[:, sls[hp]] for hp in pairs])
    for hp in pairs:
        sl = sls[hp]
        yn = dev_l[hp] * lax.rsqrt(var_l[hp] + GN_EPS) * lnw_ref[:, sl] + lnb_ref[:, sl]
        o_ref[:, sl] = ((yn + bon_l[hp] * v_l[hp]) * g_all[:, sl]).astype(o_ref.dtype)

    carry_scr[...] = p_ref[L - SUBLANES:L, :]


def _rwkv(proj, batch, seq, lp, v_first):
    vres = v_first is not None
    C = lp["w0"].shape[0]
    L = RWKV_CHUNK
    W = 3 * C + LORA_WINDOW
    row = lambda a: a.reshape(1, -1)
    full = lambda a: pl.BlockSpec(a.shape, lambda b, c: (0, 0))
    tok = pl.BlockSpec((L, C), lambda b, c: (b * (seq // L) + c, 0))
    args = [proj, lp["mu_pad"], row(lp["w0"]), lp["w2_pad"], row(lp["a0"]), lp["a2_pad"], lp["g2_pad"]]
    in_specs = [pl.BlockSpec((L, W), lambda b, c: (b * (seq // L) + c, 0))]
    in_specs += [full(a) for a in args[1:]]
    if vres:
        extra = [row(lp["v0"]), lp["v2_pad"]]
        args += extra + [v_first]
        in_specs += [full(a) for a in extra] + [tok]
    tail = [row(lp["k_k"]), row(lp["k_a"]), row(lp["r_k"]), row(lp["ln_w"]), row(lp["ln_b"])]
    args += tail
    in_specs += [full(a) for a in tail]
    y_sd = jax.ShapeDtypeStruct((batch * seq, C), BF16)
    v_sd = jax.ShapeDtypeStruct((batch * seq, C), F32)
    out = pl.pallas_call(
        functools.partial(_rwkv_kernel, chunk=L, width=C, vres=vres),
        grid=(batch, seq // L),
        in_specs=in_specs,
        out_specs=tok if vres else (tok, tok),
        out_shape=y_sd if vres else (y_sd, v_sd),
        scratch_shapes=[pltpu.VMEM((SUBLANES, W), F32),
                        pltpu.VMEM((C // LANES, LANES, LANES), F32)],
        compiler_params=_cparams(("arbitrary", "arbitrary")),
        name="rwkv7",
    )(*args)
    return (out, v_first) if vres else out


def _moba_kernel(q_ref, k_ref, v_ref, slope_ref, gain_ref, o_ref, *, n_blocks):
    BS = MOBA_BLOCK
    T = n_blocks * BS
    scale = MOBA_HEAD_DIM ** -0.5

    means = [jnp.mean(k_ref[j * BS:(j + 1) * BS, :], axis=0, keepdims=True) for j in range(n_blocks)]
    km = jnp.concatenate(means + [jnp.zeros((LANES - n_blocks, LANES), F32)], axis=0)
    gate_t = _dot3(km, q_ref[...], nt=True)[0:SUBLANES, :]
    blk = lax.broadcasted_iota(jnp.int32, (SUBLANES, T), 0)
    own_of = lax.broadcasted_iota(jnp.int32, (SUBLANES, T), 1) // BS
    past = blk < own_of
    gate_t = jnp.where(past, gate_t, -jnp.inf)
    beaten = jnp.zeros((SUBLANES, T), jnp.int32)
    for j in range(n_blocks):
        gj = gate_t[j:j + 1, :]
        ahead = (gj > gate_t) | ((gj == gate_t) & (blk > j))
        beaten = beaten + ahead.astype(jnp.int32)
    sel_t = ((beaten < MOBA_TOPK) & past).astype(F32)

    ind = (lax.broadcasted_iota(jnp.int32, (LANES, T), 1) // BS
           == lax.broadcasted_iota(jnp.int32, (LANES, T), 0)).astype(BF16)
    slope = slope_ref[:, 0:1] * LOG2E
    bias = slope * lax.broadcasted_iota(jnp.int32, (1, T), 1).astype(F32)
    causal = (lax.broadcasted_iota(jnp.int32, (BS, BS), 1)
              <= lax.broadcasted_iota(jnp.int32, (BS, BS), 0))
    zpad = jnp.zeros((LANES - SUBLANES, BS), F32)

    for own in range(n_blocks):
        lo, hi = own * BS, (own + 1) * BS
        q = (q_ref[lo:hi, :] * (scale * LOG2E)).astype(BF16)
        s_o = _bdot_nt(q, k_ref[lo:hi, :]) + bias[:, 0:BS]
        s_o = jnp.where(causal, s_o, -jnp.inf)
        m = jnp.max(s_o, axis=-1, keepdims=True)
        if own > 0:
            s_p = _bdot_nt(q, k_ref[0:lo, :]) + (bias[:, 0:lo] - slope * float(lo))
            if own > MOBA_TOPK:
                sel = jnp.concatenate([sel_t[:, lo:hi], zpad], axis=0).T.astype(BF16)
                picked = jnp.dot(sel, ind[:, 0:lo], preferred_element_type=F32) > 0.5
                s_p = jnp.where(picked, s_p, -jnp.inf)
            m = jnp.maximum(m, jnp.max(s_p, axis=-1, keepdims=True))
            e_p = jnp.exp2(s_p - m)
        e_o = jnp.exp2(s_o - m)
        denom = jnp.sum(e_o, axis=-1, keepdims=True)
        out = _bdot(e_o, v_ref[lo:hi, :])
        if own > 0:
            denom = denom + jnp.sum(e_p, axis=-1, keepdims=True)
            out = out + _bdot(e_p, v_ref[0:lo, :])
        out = out / denom
        out = out * lax.rsqrt(jnp.mean(out * out, axis=-1, keepdims=True) + RMS_EPS)
        o_ref[lo:hi, :] = (out * gain_ref[...]).astype(o_ref.dtype)


def _moba(proj, batch, seq, q_off, gain):
    width = gain.shape[0]
    heads = width // MOBA_HEAD_DIM
    nb = seq // MOBA_BLOCK
    assert nb <= SUBLANES and seq % MOBA_BLOCK == 0
    qo = q_off // LANES
    slopes = jnp.exp2(-8.0 * jnp.arange(1, heads + 1, dtype=F32) / heads)
    slopes = jnp.broadcast_to(slopes[:, None, None], (heads, 1, LANES))
    return pl.pallas_call(
        functools.partial(_moba_kernel, n_blocks=nb),
        grid=(batch, heads),
        in_specs=[pl.BlockSpec((seq, LANES), lambda b, h: (b, qo + h)),
                  pl.BlockSpec((seq, LANES), lambda b, h: (b, qo + heads + h)),
                  pl.BlockSpec((seq, LANES), lambda b, h: (b, qo + 2 * heads + h)),
                  pl.BlockSpec((None, 1, LANES), lambda b, h: (h, 0, 0)),
                  pl.BlockSpec((1, LANES), lambda b, h: (0, h))],
        out_specs=pl.BlockSpec((seq, LANES), lambda b, h: (b, h)),
        out_shape=jax.ShapeDtypeStruct((batch * seq, width), BF16),
        compiler_params=_cparams(("arbitrary", "arbitrary")),
        name="moba",
    )(proj, proj, proj, slopes, gain.reshape(1, width))


def _router_kernel(x_ref, g_ref, sh_ref, sc_ref, wr_ref, h_ref, route_ref, *, n_experts):
    h = _norm_mod(x_ref[...], g_ref[...], sh_ref[...], sc_ref[...])
    h_ref[...] = h
    logits = _dot3(h, wr_ref[...])
    col = lax.broadcasted_iota(jnp.int32, logits.shape, 1)
    neg = -jnp.inf
    logits = jnp.where(col < n_experts, logits, neg)
    m1 = jnp.max(logits, axis=-1, keepdims=True)
    i1 = jnp.min(jnp.where(logits == m1, col, LANES), axis=-1, keepdims=True)
    rest = jnp.where(col == i1, neg, logits)
    m2 = jnp.max(rest, axis=-1, keepdims=True)
    i2 = jnp.min(jnp.where(rest == m2, col, LANES), axis=-1, keepdims=True)
    e2 = jnp.exp(m2 - m1)
    wa = 1.0 / (1.0 + e2)
    wb = e2 / (1.0 + e2)
    route = jnp.where(col == 0, i1.astype(F32), 0.0)
    route = jnp.where(col == 1, i2.astype(F32), route)
    route = jnp.where(col == 2, wa, route)
    route = jnp.where(col == 3, wb, route)
    route_ref[...] = route


def _router(x2, gain, mod3, router_pad, seq, shift_idx, scale_idx, n_experts, tm=512):
    m, d = x2.shape
    sh = pl.BlockSpec((None, 1, d), lambda i: ((i * tm) // seq * 6 + shift_idx, 0, 0))
    sc = pl.BlockSpec((None, 1, d), lambda i: ((i * tm) // seq * 6 + scale_idx, 0, 0))
    return pl.pallas_call(
        functools.partial(_router_kernel, n_experts=n_experts),
        grid=(m // tm,),
        in_specs=[pl.BlockSpec((tm, d), lambda i: (i, 0)),
                  pl.BlockSpec((1, d), lambda i: (0, 0)),
                  sh, sc,
                  pl.BlockSpec((d, LANES), lambda i: (0, 0))],
        out_specs=(pl.BlockSpec((tm, d), lambda i: (i, 0)),
                   pl.BlockSpec((tm, LANES), lambda i: (i, 0))),
        out_shape=(jax.ShapeDtypeStruct((m, d), F32),
                   jax.ShapeDtypeStruct((m, LANES), F32)),
        compiler_params=_cparams(("arbitrary",)),
        name="router",
    )(x2, gain.reshape(1, d), mod3, mod3, router_pad)


def _to_bf16_kernel(x_ref, o_ref):
    o_ref[...] = x_ref[...].astype(BF16)


def _to_bf16(x, tm=1024):
    m, d = x.shape
    tm = min(tm, m)
    return pl.pallas_call(
        _to_bf16_kernel,
        grid=(m // tm,),
        in_specs=[pl.BlockSpec((tm, d), lambda i: (i, 0))],
        out_specs=pl.BlockSpec((tm, d), lambda i: (i, 0)),
        out_shape=jax.ShapeDtypeStruct((m, d), BF16),
        compiler_params=_cparams(("arbitrary",)),
        name="to_bf16",
    )(x)


def _sub_blocks(rows, o_ref, compute):
    tm = o_ref.shape[0]
    half = tm // 2

    @pl.when(rows > half)
    def _():
        o_ref[...] = compute(slice(0, tm))

    @pl.when((rows > 0) & (rows <= half))
    def _():
        o_ref[0:half, :] = compute(slice(0, half))
        o_ref[half:tm, :] = jnp.zeros((tm - half, o_ref.shape[1]), o_ref.dtype)

    @pl.when(rows == 0)
    def _():
        o_ref[...] = jnp.zeros_like(o_ref)


def _moe_up_kernel(te_ref, tr_ref, tn_ref, x_ref, wg_ref, wu_ref, o_ref, sg_ref, su_ref):
    i = pl.program_id(1)

    @pl.when(tn_ref[i] > 0)
    def _():
        _cast_block(wg_ref, sg_ref)
        _cast_block(wu_ref, su_ref)

    def compute(sl):
        x = x_ref[sl, :]
        gate = jnp.dot(x, sg_ref[...], preferred_element_type=F32)
        up = jnp.dot(x, su_ref[...], preferred_element_type=F32)
        return (jax.nn.silu(gate) * up).astype(BF16)

    _sub_blocks(tr_ref[i], o_ref, compute)


def _moe_up(tiles, xs, wg, wu, tm, tf=512):
    p, d = xs.shape
    f = wg.shape[2]
    grid_spec = pltpu.PrefetchScalarGridSpec(
        num_scalar_prefetch=3,
        grid=(f // tf, p // tm),
        in_specs=[pl.BlockSpec((tm, d), lambda j, i, te, tv, tn: (i, 0)),
                  pl.BlockSpec((None, d, tf), lambda j, i, te, tv, tn: (te[i], 0, j)),
                  pl.BlockSpec((None, d, tf), lambda j, i, te, tv, tn: (te[i], 0, j))],
        out_specs=pl.BlockSpec((tm, tf), lambda j, i, te, tv, tn: (i, j)),
        scratch_shapes=[pltpu.VMEM((d, tf), BF16), pltpu.VMEM((d, tf), BF16)],
    )
    return pl.pallas_call(
        _moe_up_kernel,
        grid_spec=grid_spec,
        out_shape=jax.ShapeDtypeStruct((p, f), BF16),
        compiler_params=_cparams(("arbitrary", "arbitrary")),
        name="moe_up",
    )(*tiles, xs, wg, wu)


def _moe_down_kernel(te_ref, tr_ref, tn_ref, a_ref, wd_ref, o_ref, sd_ref):
    i = pl.program_id(1)

    @pl.when(tn_ref[i] > 0)
    def _():
        _cast_block(wd_ref, sd_ref)

    _sub_blocks(tr_ref[i], o_ref,
                lambda sl: jnp.dot(a_ref[sl, :], sd_ref[...], preferred_element_type=F32))


def _moe_down(tiles, act, wd, tm, tn=512):
    p, f = act.shape
    n = wd.shape[2]
    grid_spec = pltpu.PrefetchScalarGridSpec(
        num_scalar_prefetch=3,
        grid=(n // tn, p // tm),
        in_specs=[pl.BlockSpec((tm, f), lambda j, i, te, tv, tn_: (i, 0)),
                  pl.BlockSpec((None, f, tn), lambda j, i, te, tv, tn_: (te[i], 0, j))],
        out_specs=pl.BlockSpec((tm, tn), lambda j, i, te, tv, tn_: (i, j)),
        scratch_shapes=[pltpu.VMEM((f, tn), BF16)],
    )
    return pl.pallas_call(
        _moe_down_kernel,
        grid_spec=grid_spec,
        out_shape=jax.ShapeDtypeStruct((p, n), F32),
        compiler_params=_cparams(("arbitrary", "arbitrary")),
        name="moe_down",
    )(*tiles, act, wd)


def _moe_combine_kernel(x_ref, ya_ref, yb_ref, route_ref, gate_ref, *rest, final):
    route = route_ref[...]
    f = route[:, 2:3] * ya_ref[...] + route[:, 3:4] * yb_ref[...]
    x = x_ref[...] + gate_ref[...] * f
    if final:
        g_ref, o_ref = rest
        x = x * lax.rsqrt(jnp.mean(x * x, axis=-1, keepdims=True) + RMS_EPS) * g_ref[...]
    else:
        o_ref, = rest
    o_ref[...] = x


def _moe_combine(x2, yab, route, mod3, seq, gate_idx, final_gain, tm=256):
    m, d = x2.shape
    final = final_gain is not None
    in_specs = [pl.BlockSpec((tm, d), lambda i: (i, 0)),
                pl.BlockSpec((tm, d), lambda i: (i, 0)),
                pl.BlockSpec((tm, d), lambda i: (i + m // tm, 0)),
                pl.BlockSpec((tm, LANES), lambda i: (i, 0)),
                pl.BlockSpec((None, 1, d), lambda i: ((i * tm) // seq * 6 + gate_idx, 0, 0))]
    args = [x2, yab, yab, route, mod3]
    if final:
        in_specs.append(pl.BlockSpec((1, d), lambda i: (0, 0)))
        args.append(final_gain.reshape(1, d))
    return pl.pallas_call(
        functools.partial(_moe_combine_kernel, final=final),
        grid=(m // tm,),
        in_specs=in_specs,
        out_specs=pl.BlockSpec((tm, d), lambda i: (i, 0)),
        out_shape=jax.ShapeDtypeStruct((m, d), F32),
        compiler_params=_cparams(("arbitrary",)),
        name="moe_combine",
    )(*args)


def _moe(x2, gain, mod3, router, wg, wu, wd, seq, final_gain, tm=512):
    m, d = x2.shape
    n_experts = router.shape[1]
    router_pad = jnp.pad(router, ((0, 0), (0, LANES - n_experts)))
    h, route = _router(x2, gain, mod3, router_pad, seq, 3, 4, n_experts)

    slots = m * TOP_K
    n_tiles = slots // tm + n_experts
    flat_e = route[:, 0:TOP_K].astype(jnp.int32).reshape(slots)
    onehot = (flat_e[:, None] == jnp.arange(n_experts)[None, :]).astype(jnp.int32)
    rank = jnp.sum((jnp.cumsum(onehot, axis=0) - onehot) * onehot, axis=1)
    counts = jnp.sum(onehot, axis=0)
    padded = (counts + tm - 1) // tm * tm
    ends = jnp.cumsum(padded)
    dest = jnp.sum(onehot * (ends - padded)[None, :], axis=1) + rank
    sorted_tok = (jnp.arange(n_tiles * tm, dtype=jnp.int32) % m).at[dest].set(
        jnp.arange(slots, dtype=jnp.int32) // TOP_K)
    tile_start = jnp.arange(n_tiles, dtype=jnp.int32) * tm
    tile_expert = jnp.minimum(jnp.sum((tile_start[:, None] >= ends[None, :]).astype(jnp.int32), axis=1),
                              n_experts - 1)
    tile_rows = jnp.clip((ends - padded + counts)[tile_expert] - tile_start, 0, tm).astype(jnp.int32)
    tile_new = jnp.concatenate([jnp.ones((1,), jnp.int32),
                                (tile_expert[1:] != tile_expert[:-1]).astype(jnp.int32)])
    tiles = (tile_expert, tile_rows, tile_new)

    xs = _to_bf16(h.at[sorted_tok].get(mode="promise_in_bounds"))
    act = _moe_up(tiles, xs, wg, wu, tm)
    ys = _moe_down(tiles, act, wd, tm)
    dest_slot_major = dest.reshape(m, TOP_K).T.reshape(slots)
    yab = ys.at[dest_slot_major].get(mode="promise_in_bounds")
    return _moe_combine(x2, yab, route, mod3, seq, 5, final_gain)


def _final_norm_kernel(x_ref, g_ref, o_ref):
    x = x_ref[...]
    o_ref[...] = x * lax.rsqrt(jnp.mean(x * x, axis=-1, keepdims=True) + RMS_EPS) * g_ref[...]


def _final_norm(x2, gain, tm=512):
    m, d = x2.shape
    return pl.pallas_call(
        _final_norm_kernel,
        grid=(m // tm,),
        in_specs=[pl.BlockSpec((tm, d), lambda i: (i, 0)),
                  pl.BlockSpec((1, d), lambda i: (0, 0))],
        out_specs=pl.BlockSpec((tm, d), lambda i: (i, 0)),
        out_shape=jax.ShapeDtypeStruct((m, d), F32),
        compiler_params=_cparams(("arbitrary",)),
        name="final_norm",
    )(x2, gain.reshape(1, d))


def _prep_mixer(lp, C):
    w_in, mu = lp["w_in"], lp["shift_mu"]
    n_r = mu.shape[0]
    assert 3 * C < n_r <= 3 * C + LORA_WINDOW <= RWKV_PROJ_COLS
    out = dict(lp)
    out["w_in_r"] = w_in[:, :RWKV_PROJ_COLS].astype(BF16)
    out["w_in_m"] = w_in[:, n_r:].astype(BF16)
    out["mu_pad"] = jnp.pad(mu, (0, 3 * C + LORA_WINDOW - n_r)).reshape(1, -1)
    off = 0
    for name in ("w2", "a2", "g2") + (("v2",) if "v2" in lp else ()):
        rank = lp[name].shape[0]
        out[name + "_pad"] = jnp.pad(lp[name], ((off, LORA_WINDOW - off - rank), (0, 0)))
        off += rank
    assert 3 * C + off == n_r
    return out


def _layer(x2, mod3, lp, batch, seq, v_first, final_gain):
    C = lp["w0"].shape[0]
    lp = _prep_mixer(lp, C)
    h = _normmod(x2, lp["norm_mix"], mod3, seq, 0, 1)
    y_r, v_first = _rwkv(_in_proj(h, lp["w_in_r"]), batch, seq, lp, v_first)
    y_m = _moba(_in_proj(h, lp["w_in_m"]), batch, seq, 0, lp["moba_gain"])
    x2 = _mm_res([y_r, y_m], lp["w_out"], x2, mod3, seq, 2, tm=1024, tn=1024)
    if "ffn_gate" in lp:
        act = _ffn_up(_normmod(x2, lp["norm_ffn"], mod3, seq, 3, 4), lp["ffn_gate"], lp["ffn_up"])
        x2 = _mm_res([act], lp["ffn_down"], x2, mod3, seq, 5)
        if final_gain is not None:
            x2 = _final_norm(x2, final_gain)
    else:
        x2 = _moe(x2, lp["norm_ffn"], mod3, lp["router"], lp["exp_gate"], lp["exp_up"],
                  lp["exp_down"], seq, final_gain)
    return x2, v_first


def _forward(x, c, layers, norm_out):
    batch, seq, d = x.shape
    x2 = x.reshape(batch * seq, d)
    c_pad = jnp.pad(c, ((0, SUBLANES - batch), (0, 0)))
    v_first = None
    for li, lp in enumerate(layers):
        mod = _adaln(c_pad, lp["mod_w"], lp["mod_b"])[:batch]
        mod3 = mod.reshape(batch * 6, 1, d)
        last = li == len(layers) - 1
        x2, v_first = _layer(x2, mod3, lp, batch, seq, v_first, norm_out if last else None)
    return x2.reshape(batch, seq, d)


def kernel(x, c, l0_mod_w, l0_mod_b, l0_norm_mix, l0_w_in, l0_shift_mu, l0_w0, l0_w2, l0_a0, l0_a2, l0_g2, l0_k_k, l0_k_a, l0_r_k, l0_ln_w, l0_ln_b, l0_moba_gain, l0_w_out, l0_norm_ffn, l0_ffn_gate, l0_ffn_up, l0_ffn_down, l1_mod_w, l1_mod_b, l1_norm_mix, l1_w_in, l1_shift_mu, l1_w0, l1_w2, l1_a0, l1_a2, l1_g2, l1_v0, l1_v2, l1_k_k, l1_k_a, l1_r_k, l1_ln_w, l1_ln_b, l1_moba_gain, l1_w_out, l1_norm_ffn, l1_router, l1_exp_gate, l1_exp_up, l1_exp_down, norm_out):
    layers = (
        dict(mod_w=l0_mod_w, mod_b=l0_mod_b, norm_mix=l0_norm_mix, w_in=l0_w_in, shift_mu=l0_shift_mu,
             w0=l0_w0, w2=l0_w2, a0=l0_a0, a2=l0_a2, g2=l0_g2, k_k=l0_k_k, k_a=l0_k_a, r_k=l0_r_k,
             ln_w=l0_ln_w, ln_b=l0_ln_b, moba_gain=l0_moba_gain, w_out=l0_w_out, norm_ffn=l0_norm_ffn,
             ffn_gate=l0_ffn_gate, ffn_up=l0_ffn_up, ffn_down=l0_ffn_down),
        dict(mod_w=l1_mod_w, mod_b=l1_mod_b, norm_mix=l1_norm_mix, w_in=l1_w_in, shift_mu=l1_shift_mu,
             w0=l1_w0, w2=l1_w2, a0=l1_a0, a2=l1_a2, g2=l1_g2, v0=l1_v0, v2=l1_v2, k_k=l1_k_k,
             k_a=l1_k_a, r_k=l1_r_k, ln_w=l1_ln_w, ln_b=l1_ln_b, moba_gain=l1_moba_gain, w_out=l1_w_out,
             norm_ffn=l1_norm_ffn, router=l1_router, exp_gate=l1_exp_gate, exp_up=l1_exp_up,
             exp_down=l1_exp_down),
    )
    return _forward(x, c, layers, norm_out)
```

```python
import functools

import jax
import jax.numpy as jnp
from jax import lax
from jax.experimental import pallas as pl
from jax.experimental.pallas import tpu as pltpu

F32 = jnp.float32
BF16 = jnp.bfloat16

RMS_EPS = 1e-6
GN_EPS = 64e-5
RWKV_HEAD_DIM = 64
MOBA_HEAD_DIM = 128
MOBA_BLOCK = 256
MOBA_TOPK = 3
TOP_K = 2
LANES = 128
SUBLANES = 8
RWKV_CHUNK = 64
VMEM_LIMIT = 52 * 1024 * 1024
CAST_ROWS = 256
MOE_CAST_SLAB = 512
LOG2E = 1.4426950408889634

LORA_WINDOW = 384
RWKV_PROJ_COLS = 3584


def _cparams(sem):
    return pltpu.CompilerParams(dimension_semantics=sem, vmem_limit_bytes=VMEM_LIMIT)


def _bdot(a, b):
    return jnp.dot(a.astype(BF16), b.astype(BF16), preferred_element_type=F32)


def _bdot_nt(a, b):
    return lax.dot_general(a.astype(BF16), b.astype(BF16), (((1,), (1,)), ((), ())),
                           preferred_element_type=F32)


def _bdot_tn(a, b):
    return lax.dot_general(a.astype(BF16), b.astype(BF16), (((0,), (0,)), ((), ())),
                           preferred_element_type=F32)


def _split(x, terms):
    parts = []
    rem = x
    for _ in range(terms):
        hi = rem.astype(BF16)
        parts.append(hi)
        rem = rem - hi.astype(F32)
    return parts


def _dot_xe(x, e_bf16, terms):
    acc = None
    for part in _split(x, terms):
        d = jnp.dot(part, e_bf16, preferred_element_type=F32)
        acc = d if acc is None else acc + d
    return acc


def _dot_ex(e_bf16, x, terms):
    acc = None
    for part in _split(x, terms):
        d = jnp.dot(e_bf16, part, preferred_element_type=F32)
        acc = d if acc is None else acc + d
    return acc


def _dot3(a, b, nt=False):
    a_hi, a_lo = _split(a, 2)
    b_hi, b_lo = _split(b, 2)
    if nt:
        f = lambda u, v: lax.dot_general(u, v, (((1,), (1,)), ((), ())), preferred_element_type=F32)
    else:
        f = lambda u, v: jnp.dot(u, v, preferred_element_type=F32)
    return f(a_hi, b_hi) + (f(a_lo, b_hi) + f(a_hi, b_lo))


def _norm_mod(x, gain, shift, scale):
    y = x * lax.rsqrt(jnp.mean(x * x, axis=-1, keepdims=True) + RMS_EPS)
    return (y * gain) * (1.0 + scale) + shift


def _cast_block(w_ref, s_ref):
    if w_ref.dtype == BF16:
        return
    rows = w_ref.shape[0]
    step = CAST_ROWS if rows % CAST_ROWS == 0 else rows

    def body(r, carry):
        sl = pl.ds(pl.multiple_of(r * step, step), step)
        s_ref[sl, :] = w_ref[sl, :].astype(BF16)
        return carry

    lax.fori_loop(0, rows // step, body, 0)


def _mod_kernel(c_ref, w_ref, b_ref, o_ref):
    c = c_ref[...]
    o_ref[...] = _dot3(c * jax.nn.sigmoid(c), w_ref[...]) + b_ref[...]


def _adaln(c_pad, mod_w, mod_b, tn=512):
    m, d = c_pad.shape
    n = mod_w.shape[1]
    return pl.pallas_call(
        _mod_kernel,
        grid=(n // tn,),
        in_specs=[pl.BlockSpec((m, d), lambda j: (0, 0)),
                  pl.BlockSpec((d, tn), lambda j: (0, j)),
                  pl.BlockSpec((1, tn), lambda j: (0, j))],
        out_specs=pl.BlockSpec((m, tn), lambda j: (0, j)),
        out_shape=jax.ShapeDtypeStruct((m, n), F32),
        compiler_params=_cparams(("arbitrary",)),
        name="adaln",
    )(c_pad, mod_w, mod_b.reshape(1, n))


def _normmod_kernel(x_ref, g_ref, sh_ref, sc_ref, o_ref):
    o_ref[...] = _norm_mod(x_ref[...], g_ref[...], sh_ref[...], sc_ref[...]).astype(BF16)


def _normmod(x2, gain, mod3, seq, shift_idx, scale_idx, tm=512):
    m, d = x2.shape
    return pl.pallas_call(
        _normmod_kernel,
        grid=(m // tm,),
        in_specs=[pl.BlockSpec((tm, d), lambda i: (i, 0)),
                  pl.BlockSpec((1, d), lambda i: (0, 0)),
                  pl.BlockSpec((None, 1, d), lambda i: ((i * tm) // seq * 6 + shift_idx, 0, 0)),
                  pl.BlockSpec((None, 1, d), lambda i: ((i * tm) // seq * 6 + scale_idx, 0, 0))],
        out_specs=pl.BlockSpec((tm, d), lambda i: (i, 0)),
        out_shape=jax.ShapeDtypeStruct((m, d), BF16),
        compiler_params=_cparams(("arbitrary",)),
        name="normmod",
    )(x2, gain.reshape(1, d), mod3, mod3)


def _in_proj_kernel(h_ref, w_ref, o_ref):
    o_ref[...] = jnp.dot(h_ref[...], w_ref[...], preferred_element_type=F32)


def _in_proj(h, w, tm=1024):
    m, d = h.shape
    n = w.shape[1]
    tm = min(tm, m)
    tn = n // 2
    assert tn % (2 * LANES) == 0
    return pl.pallas_call(
        _in_proj_kernel,
        grid=(n // tn, m // tm),
        in_specs=[pl.BlockSpec((tm, d), lambda j, i: (i, 0)),
                  pl.BlockSpec((d, tn), lambda j, i: (0, j))],
        out_specs=pl.BlockSpec((tm, tn), lambda j, i: (i, j)),
        out_shape=jax.ShapeDtypeStruct((m, n), F32),
        compiler_params=_cparams(("arbitrary", "arbitrary")),
        name="in_proj",
    )(h, w)


def _ffn_up_kernel(h_ref, wg_ref, wu_ref, o_ref, sg_ref, su_ref):
    @pl.when(pl.program_id(1) == 0)
    def _():
        _cast_block(wg_ref, sg_ref)
        _cast_block(wu_ref, su_ref)

    h = h_ref[...]
    gate = jnp.dot(h, sg_ref[...], preferred_element_type=F32)
    up = jnp.dot(h, su_ref[...], preferred_element_type=F32)
    o_ref[...] = (jax.nn.silu(gate) * up).astype(BF16)


def _ffn_up(h, wg, wu, tm=1024, tn=512):
    m, d = h.shape
    tm = min(tm, m)
    n = wg.shape[1]
    return pl.pallas_call(
        _ffn_up_kernel,
        grid=(n // tn, m // tm),
        in_specs=[pl.BlockSpec((tm, d), lambda j, i: (i, 0)),
                  pl.BlockSpec((d, tn), lambda j, i: (0, j)),
                  pl.BlockSpec((d, tn), lambda j, i: (0, j))],
        out_specs=pl.BlockSpec((tm, tn), lambda j, i: (i, j)),
        out_shape=jax.ShapeDtypeStruct((m, n), BF16),
        scratch_shapes=[pltpu.VMEM((d, tn), BF16), pltpu.VMEM((d, tn), BF16)],
        compiler_params=_cparams(("arbitrary", "arbitrary")),
        name="ffn_up",
    )(h, wg, wu)


def _mm_res_kernel(*refs, n_lhs):
    lhs = refs[:n_lhs]
    ws = refs[n_lhs:2 * n_lhs]
    res_ref, gate_ref, o_ref = refs[2 * n_lhs:2 * n_lhs + 3]
    scr = refs[2 * n_lhs + 3:]

    @pl.when(pl.program_id(1) == 0)
    def _():
        for w_ref, s_ref in zip(ws, scr):
            _cast_block(w_ref, s_ref)

    acc = None
    for l_ref, s_ref in zip(lhs, scr):
        d = jnp.dot(l_ref[...], s_ref[...], preferred_element_type=F32)
        acc = d if acc is None else acc + d
    o_ref[...] = res_ref[...] + gate_ref[...] * acc


def _mm_res(lhs_list, w, res, mod3, seq, gate_idx, tm=512, tn=512):
    m, n = res.shape
    tm = min(tm, m)
    k = lhs_list[0].shape[1]
    n_lhs = len(lhs_list)
    in_specs = [pl.BlockSpec((tm, k), lambda j, i: (i, 0)) for _ in lhs_list]
    in_specs += [pl.BlockSpec((k, tn), functools.partial(lambda j, i, li: (li, j), li=li))
                 for li in range(n_lhs)]
    in_specs += [pl.BlockSpec((tm, tn), lambda j, i: (i, j)),
                 pl.BlockSpec((None, 1, tn), lambda j, i: ((i * tm) // seq * 6 + gate_idx, 0, j))]
    return pl.pallas_call(
        functools.partial(_mm_res_kernel, n_lhs=n_lhs),
        grid=(n // tn, m // tm),
        in_specs=in_specs,
        out_specs=pl.BlockSpec((tm, tn), lambda j, i: (i, j)),
        out_shape=jax.ShapeDtypeStruct((m, n), F32),
        scratch_shapes=[pltpu.VMEM((k, tn), BF16) for _ in lhs_list],
        compiler_params=_cparams(("arbitrary", "arbitrary")),
        name="mm_res",
    )(*lhs_list, *([w] * n_lhs), res, mod3)


def _rwkv_kernel(*refs, chunk, width, vres):
    it = iter(refs)
    p_ref, mu_ref, w0_ref, w2_ref, a0_ref, a2_ref, g2_ref = (next(it) for _ in range(7))
    if vres:
        v0_ref, v2_ref, vf_ref = next(it), next(it), next(it)
    kk_ref, ka_ref, rk_ref, lnw_ref, lnb_ref = (next(it) for _ in range(5))
    o_ref = next(it)
    vf_out_ref = None if vres else next(it)
    carry_scr, state_scr = next(it), next(it)

    L, C = chunk, width
    P2 = 2 * L
    pairs = list(range(C // LANES))
    ci = pl.program_id(1)

    @pl.when(ci == 0)
    def _():
        carry_scr[...] = jnp.zeros_like(carry_scr)
        state_scr[...] = jnp.zeros_like(state_scr)

    def shifted(lo, hi):
        p = p_ref[:, lo:hi]
        rolled = pltpu.roll(p, 1, 0)
        row = lax.broadcasted_iota(jnp.int32, p.shape, 0)
        prev = jnp.where(row == 0, carry_scr[SUBLANES - 1:SUBLANES, lo:hi], rolled)
        return p + (prev - p) * mu_ref[:, lo:hi]

    zl = shifted(3 * C, 3 * C + LORA_WINDOW)
    wl = w0_ref[...] + _bdot(jnp.tanh(zl), w2_ref[...])
    logdecay = -jnp.exp(-jax.nn.softplus(-wl) - 0.5)
    a_all = jax.nn.sigmoid(a0_ref[...] + _bdot(zl, a2_ref[...]))
    g_all = _bdot(jax.nn.sigmoid(zl), g2_ref[...])
    if vres:
        vmix = jax.nn.sigmoid(v0_ref[...] + _bdot(zl, v2_ref[...]))

    tr = lax.broadcasted_iota(jnp.int32, (L, L), 0)
    tc = lax.broadcasted_iota(jnp.int32, (L, L), 1)
    tri = (tc <= tr).astype(BF16)
    cum = _dot_ex(tri, logdecay, 3)
    p_inc = jnp.exp(cum)
    p_exc = jnp.exp(cum - logdecay)
    p_inv = jnp.exp(-cum)

    rr = lax.broadcasted_iota(jnp.int32, (P2, P2), 0)
    cc = lax.broadcasted_iota(jnp.int32, (P2, P2), 1)
    strict = cc < rr
    incl = cc <= rr
    eye = (cc == rr).astype(F32)
    lane = lax.broadcasted_iota(jnp.int32, (L, LANES), 1)
    head0 = lane < RWKV_HEAD_DIM
    seg = ((rr // RWKV_HEAD_DIM) == (cc // RWKV_HEAD_DIM)).astype(BF16)

    def expand(a):
        return jnp.concatenate([jnp.where(head0, a, 0.0), jnp.where(head0, 0.0, a)], axis=0)

    sls = [slice(hp * LANES, (hp + 1) * LANES) for hp in pairs]
    def seg_sums(xs):
        tot = _dot_xe(jnp.concatenate(xs, axis=0), seg, 2)
        return [tot[i * L:(i + 1) * L] for i in range(len(xs))]

    r_l, k_l, v_l, kk0_l = [], [], [], []
    for hp in pairs:
        sl = sls[hp]
        r_l.append(shifted(hp * LANES, (hp + 1) * LANES))
        k_l.append(shifted(C + hp * LANES, C + (hp + 1) * LANES))
        v = shifted(2 * C + hp * LANES, 2 * C + (hp + 1) * LANES)
        if vres:
            v = v + (vf_ref[:, sl] - v) * vmix[:, sl]
        else:
            vf_out_ref[:, sl] = v
        v_l.append(v)
        kk0 = k_l[hp] * kk_ref[:, sl]
        kk0_l.append(kk0)
    ssq_l = seg_sums([kk0 * kk0 for kk0 in kk0_l])

    lhs_l, rhs_l, ax_l, vx_l, k2_l = [], [], [], [], []
    for hp in pairs:
        sl = sls[hp]
        a = a_all[:, sl]
        kk = kk0_l[hp] * lax.rsqrt(jnp.maximum(ssq_l[hp], 1e-24))
        k2 = k_l[hp] * (1.0 + (a - 1.0) * ka_ref[:, sl])
        k2_l.append(k2)
        ax = expand(-kk * p_exc[:, sl]).astype(BF16)
        rx = expand(r_l[hp] * p_inc[:, sl]).astype(BF16)
        bx = expand(kk * a * p_inv[:, sl]).astype(BF16)
        kx = expand(k2 * p_inv[:, sl]).astype(BF16)
        ax_l.append(ax)
        vx_l.append(expand(v_l[hp]).astype(BF16))
        lhs_l.append(jnp.concatenate([ax, rx], axis=0))
        rhs_l.append(jnp.concatenate([bx, kx], axis=0))

    gram_l = [_bdot_nt(lhs_l[hp], rhs_l[hp]) for hp in pairs]
    a_ak_l = [jnp.where(strict, g[:P2, P2:], 0.0).astype(BF16) for g in gram_l]
    a_r_l = [jnp.concatenate([jnp.where(incl, g[P2:, :P2], 0.0),
                              jnp.where(incl, g[P2:, P2:], 0.0)], axis=1).astype(BF16) for g in gram_l]

    n_l = [jnp.where(strict, g[:P2, :P2], 0.0) for g in gram_l]
    tinv_l = [eye + n for n in n_l]
    n_l = [_bdot(n, n) for n in n_l]
    span = 4
    while span < L:
        prod_l = [_bdot(jnp.concatenate([t, n], axis=0), n) for t, n in zip(tinv_l, n_l)]
        tinv_l = [t + p[:P2] for t, p in zip(tinv_l, prod_l)]
        n_l = [p[P2:] for p in prod_l]
        span *= 2
    tinv_l = [t + _bdot(t, n) for t, n in zip(tinv_l, n_l)]

    akv_l = [_bdot(a_ak_l[hp], vx_l[hp]) for hp in pairs]
    w_l = [_bdot(tinv_l[hp], jnp.concatenate([ax_l[hp], akv_l[hp].astype(BF16)], axis=1)) for hp in pairs]

    s0_l = [state_scr[hp] for hp in pairs]
    z_l = [_bdot_nt(jnp.concatenate([w_l[hp][:, :LANES].astype(BF16), lhs_l[hp][P2:]], axis=0), s0_l[hp])
           for hp in pairs]
    uv_l = [jnp.concatenate([(z_l[hp][:P2] + w_l[hp][:, LANES:]).astype(BF16), vx_l[hp]], axis=0)
            for hp in pairs]
    y2_l = [z_l[hp][P2:] + _bdot(a_r_l[hp], uv_l[hp]) for hp in pairs]
    for hp in pairs:
        s_new = s0_l[hp] + _bdot_tn(uv_l[hp], rhs_l[hp])
        state_scr[hp] = s_new * p_inc[L - 1:L, sls[hp]]

    y_l = [y2[:L] + y2[L:] for y2 in y2_l]
    mean_l = [t * (1.0 / RWKV_HEAD_DIM) for t in seg_sums(y_l)]
    dev_l = [y - mu for y, mu in zip(y_l, mean_l)]
    var_l = [t * (1.0 / RWKV_HEAD_DIM) for t in seg_sums([d * d for d in dev_l])]
    bon_l = seg_sums([r_l[hp] * k2_l[hp] * rk_ref[:, sls[hp]] for hp in pairs])
    for hp in pairs:
        sl = sls[hp]
        yn = dev_l[hp] * lax.rsqrt(var_l[hp] + GN_EPS) * lnw_ref[:, sl] + lnb_ref[:, sl]
        o_ref[:, sl] = ((yn + bon_l[hp] * v_l[hp]) * g_all[:, sl]).astype(o_ref.dtype)

    carry_scr[...] = p_ref[L - SUBLANES:L, :]


def _rwkv(proj, batch, seq, lp, v_first):
    vres = v_first is not None
    C = lp["w0"].shape[0]
    L = RWKV_CHUNK
    W = 3 * C + LORA_WINDOW
    row = lambda a: a.reshape(1, -1)
    full = lambda a: pl.BlockSpec(a.shape, lambda b, c: (0, 0))
    tok = pl.BlockSpec((L, C), lambda b, c: (b * (seq // L) + c, 0))
    args = [proj, lp["mu_pad"], row(lp["w0"]), lp["w2_pad"], row(lp["a0"]), lp["a2_pad"], lp["g2_pad"]]
    in_specs = [pl.BlockSpec((L, W), lambda b, c: (b * (seq // L) + c, 0))]
    in_specs += [full(a) for a in args[1:]]
    if vres:
        extra = [row(lp["v0"]), lp["v2_pad"]]
        args += extra + [v_first]
        in_specs += [full(a) for a in extra] + [tok]
    tail = [row(lp["k_k"]), row(lp["k_a"]), row(lp["r_k"]), row(lp["ln_w"]), row(lp["ln_b"])]
    args += tail
    in_specs += [full(a) for a in tail]
    y_sd = jax.ShapeDtypeStruct((batch * seq, C), BF16)
    v_sd = jax.ShapeDtypeStruct((batch * seq, C), F32)
    out = pl.pallas_call(
        functools.partial(_rwkv_kernel, chunk=L, width=C, vres=vres),
        grid=(batch, seq // L),
        in_specs=in_specs,
        out_specs=tok if vres else (tok, tok),
        out_shape=y_sd if vres else (y_sd, v_sd),
        scratch_shapes=[pltpu.VMEM((SUBLANES, W), F32),
                        pltpu.VMEM((C // LANES, LANES, LANES), F32)],
        compiler_params=_cparams(("arbitrary", "arbitrary")),
        name="rwkv7",
    )(*args)
    return (out, v_first) if vres else out


def _moba_kernel(q_ref, k_ref, v_ref, slope_ref, gain_ref, o_ref, *, n_blocks):
    BS = MOBA_BLOCK
    T = n_blocks * BS
    scale = MOBA_HEAD_DIM ** -0.5

    means = [jnp.mean(k_ref[j * BS:(j + 1) * BS, :], axis=0, keepdims=True) for j in range(n_blocks)]
    km = jnp.concatenate(means + [jnp.zeros((LANES - n_blocks, LANES), F32)], axis=0)
    gate_t = _dot3(km, q_ref[...], nt=True)[0:SUBLANES, :]
    blk = lax.broadcasted_iota(jnp.int32, (SUBLANES, T), 0)
    own_of = lax.broadcasted_iota(jnp.int32, (SUBLANES, T), 1) // BS
    past = blk < own_of
    gate_t = jnp.where(past, gate_t, -jnp.inf)
    beaten = jnp.zeros((SUBLANES, T), jnp.int32)
    for j in range(n_blocks):
        gj = gate_t[j:j + 1, :]
        ahead = (gj > gate_t) | ((gj == gate_t) & (blk > j))
        beaten = beaten + ahead.astype(jnp.int32)
    sel_t = ((beaten < MOBA_TOPK) & past).astype(F32)

    ind = (lax.broadcasted_iota(jnp.int32, (LANES, T), 1) // BS
           == lax.broadcasted_iota(jnp.int32, (LANES, T), 0)).astype(BF16)
    slope = slope_ref[:, 0:1] * LOG2E
    bias = slope * lax.broadcasted_iota(jnp.int32, (1, T), 1).astype(F32)
    causal = (lax.broadcasted_iota(jnp.int32, (BS, BS), 1)
              <= lax.broadcasted_iota(jnp.int32, (BS, BS), 0))
    zpad = jnp.zeros((LANES - SUBLANES, BS), F32)

    for own in range(n_blocks):
        lo, hi = own * BS, (own + 1) * BS
        q = (q_ref[lo:hi, :] * (scale * LOG2E)).astype(BF16)
        s_o = _bdot_nt(q, k_ref[lo:hi, :]) + bias[:, 0:BS]
        s_o = jnp.where(causal, s_o, -jnp.inf)
        m = jnp.max(s_o, axis=-1, keepdims=True)
        if own > 0:
            s_p = _bdot_nt(q, k_ref[0:lo, :]) + (bias[:, 0:lo] - slope * float(lo))
            if own > MOBA_TOPK:
                sel = jnp.concatenate([sel_t[:, lo:hi], zpad], axis=0).T.astype(BF16)
                picked = jnp.dot(sel, ind[:, 0:lo], preferred_element_type=F32) > 0.5
                s_p = jnp.where(picked, s_p, -jnp.inf)
            m = jnp.maximum(m, jnp.max(s_p, axis=-1, keepdims=True))
            e_p = jnp.exp2(s_p - m)
        e_o = jnp.exp2(s_o - m)
        denom = jnp.sum(e_o, axis=-1, keepdims=True)
        out = _bdot(e_o, v_ref[lo:hi, :])
        if own > 0:
            denom = denom + jnp.sum(e_p, axis=-1, keepdims=True)
            out = out + _bdot(e_p, v_ref[0:lo, :])
        out = out / denom
        out = out * lax.rsqrt(jnp.mean(out * out, axis=-1, keepdims=True) + RMS_EPS)
        o_ref[lo:hi, :] = (out * gain_ref[...]).astype(o_ref.dtype)


def _moba(proj, batch, seq, q_off, gain):
    width = gain.shape[0]
    heads = width // MOBA_HEAD_DIM
    nb = seq // MOBA_BLOCK
    assert nb <= SUBLANES and seq % MOBA_BLOCK == 0
    qo = q_off // LANES
    slopes = jnp.exp2(-8.0 * jnp.arange(1, heads + 1, dtype=F32) / heads)
    slopes = jnp.broadcast_to(slopes[:, None, None], (heads, 1, LANES))
    return pl.pallas_call(
        functools.partial(_moba_kernel, n_blocks=nb),
        grid=(batch, heads),
        in_specs=[pl.BlockSpec((seq, LANES), lambda b, h: (b, qo + h)),
                  pl.BlockSpec((seq, LANES), lambda b, h: (b, qo + heads + h)),
                  pl.BlockSpec((seq, LANES), lambda b, h: (b, qo + 2 * heads + h)),
                  pl.BlockSpec((None, 1, LANES), lambda b, h: (h, 0, 0)),
                  pl.BlockSpec((1, LANES), lambda b, h: (0, h))],
        out_specs=pl.BlockSpec((seq, LANES), lambda b, h: (b, h)),
        out_shape=jax.ShapeDtypeStruct((batch * seq, width), BF16),
        compiler_params=_cparams(("arbitrary", "arbitrary")),
        name="moba",
    )(proj, proj, proj, slopes, gain.reshape(1, width))


def _router_kernel(x_ref, g_ref, sh_ref, sc_ref, wr_ref, h_ref, route_ref, *, n_experts):
    h = _norm_mod(x_ref[...], g_ref[...], sh_ref[...], sc_ref[...])
    h_ref[...] = h
    logits = _dot3(h, wr_ref[...])
    col = lax.broadcasted_iota(jnp.int32, logits.shape, 1)
    neg = -jnp.inf
    logits = jnp.where(col < n_experts, logits, neg)
    m1 = jnp.max(logits, axis=-1, keepdims=True)
    i1 = jnp.min(jnp.where(logits == m1, col, LANES), axis=-1, keepdims=True)
    rest = jnp.where(col == i1, neg, logits)
    m2 = jnp.max(rest, axis=-1, keepdims=True)
    i2 = jnp.min(jnp.where(rest == m2, col, LANES), axis=-1, keepdims=True)
    e2 = jnp.exp(m2 - m1)
    wa = 1.0 / (1.0 + e2)
    wb = e2 / (1.0 + e2)
    route = jnp.where(col == 0, i1.astype(F32), 0.0)
    route = jnp.where(col == 1, i2.astype(F32), route)
    route = jnp.where(col == 2, wa, route)
    route = jnp.where(col == 3, wb, route)
    route_ref[...] = route


def _router(x2, gain, mod3, router_pad, seq, shift_idx, scale_idx, n_experts, tm=512):
    m, d = x2.shape
    sh = pl.BlockSpec((None, 1, d), lambda i: ((i * tm) // seq * 6 + shift_idx, 0, 0))
    sc = pl.BlockSpec((None, 1, d), lambda i: ((i * tm) // seq * 6 + scale_idx, 0, 0))
    return pl.pallas_call(
        functools.partial(_router_kernel, n_experts=n_experts),
        grid=(m // tm,),
        in_specs=[pl.BlockSpec((tm, d), lambda i: (i, 0)),
                  pl.BlockSpec((1, d), lambda i: (0, 0)),
                  sh, sc,
                  pl.BlockSpec((d, LANES), lambda i: (0, 0))],
        out_specs=(pl.BlockSpec((tm, d), lambda i: (i, 0)),
                   pl.BlockSpec((tm, LANES), lambda i: (i, 0))),
        out_shape=(jax.ShapeDtypeStruct((m, d), F32),
                   jax.ShapeDtypeStruct((m, LANES), F32)),
        compiler_params=_cparams(("arbitrary",)),
        name="router",
    )(x2, gain.reshape(1, d), mod3, mod3, router_pad)


def _to_bf16_kernel(x_ref, o_ref):
    o_ref[...] = x_ref[...].astype(BF16)


def _to_bf16(x, tm=1024):
    m, d = x.shape
    tm = min(tm, m)
    return pl.pallas_call(
        _to_bf16_kernel,
        grid=(m // tm,),
        in_specs=[pl.BlockSpec((tm, d), lambda i: (i, 0))],
        out_specs=pl.BlockSpec((tm, d), lambda i: (i, 0)),
        out_shape=jax.ShapeDtypeStruct((m, d), BF16),
        compiler_params=_cparams(("arbitrary",)),
        name="to_bf16",
    )(x)


def _tile_rows(rows, ok, o_ref, compute):
    tm = o_ref.shape[0]
    half = tm // 2

    @pl.when(ok & (rows > half))
    def _():
        o_ref[...] = compute(slice(0, tm))

    @pl.when(ok & (rows > 0) & (rows <= half))
    def _():
        o_ref[0:half, :] = compute(slice(0, half))
        o_ref[half:tm, :] = jnp.zeros((tm - half, o_ref.shape[1]), o_ref.dtype)

    @pl.when(ok & (rows == 0))
    def _():
        o_ref[...] = jnp.zeros_like(o_ref)


def _dots_casting(x, w_refs, s_refs):
    accs = [None] * len(w_refs)
    for k0 in range(0, x.shape[1], MOE_CAST_SLAB):
        ks = slice(k0, k0 + MOE_CAST_SLAB)
        for n, (w_ref, s_ref) in enumerate(zip(w_refs, s_refs)):
            wk = w_ref[ks, :].astype(BF16)
            s_ref[ks, :] = wk
            d = jnp.dot(x[:, ks], wk, preferred_element_type=F32)
            accs[n] = d if accs[n] is None else accs[n] + d
    return accs


def _moe_up_kernel(te_ref, tr_ref, tn_ref, x_ref, wg_ref, wu_ref, o_ref, sg_ref, su_ref):
    i = pl.program_id(1)
    first = tn_ref[i] > 0

    @pl.when(first)
    def _():
        gate, up = _dots_casting(x_ref[...], (wg_ref, wu_ref), (sg_ref, su_ref))
        o_ref[...] = (jax.nn.silu(gate) * up).astype(BF16)

    def compute(sl):
        x = x_ref[sl, :]
        gate = jnp.dot(x, sg_ref[...], preferred_element_type=F32)
        up = jnp.dot(x, su_ref[...], preferred_element_type=F32)
        return (jax.nn.silu(gate) * up).astype(BF16)

    _tile_rows(tr_ref[i], jnp.logical_not(first), o_ref, compute)


def _moe_up(tiles, xs, wg, wu, tm, tf=512):
    p, d = xs.shape
    f = wg.shape[2]
    grid_spec = pltpu.PrefetchScalarGridSpec(
        num_scalar_prefetch=3,
        grid=(f // tf, p // tm),
        in_specs=[pl.BlockSpec((tm, d), lambda j, i, te, tv, tn: (i, 0)),
                  pl.BlockSpec((None, d, tf), lambda j, i, te, tv, tn: (te[i], 0, j)),
                  pl.BlockSpec((None, d, tf), lambda j, i, te, tv, tn: (te[i], 0, j))],
        out_specs=pl.BlockSpec((tm, tf), lambda j, i, te, tv, tn: (i, j)),
        scratch_shapes=[pltpu.VMEM((d, tf), BF16), pltpu.VMEM((d, tf), BF16)],
    )
    return pl.pallas_call(
        _moe_up_kernel,
        grid_spec=grid_spec,
        out_shape=jax.ShapeDtypeStruct((p, f), BF16),
        compiler_params=_cparams(("arbitrary", "arbitrary")),
        name="moe_up",
    )(*tiles, xs, wg, wu)


def _moe_down_kernel(te_ref, tr_ref, tn_ref, a_ref, wd_ref, o_ref, sd_ref):
    i = pl.program_id(1)
    first = tn_ref[i] > 0

    @pl.when(first)
    def _():
        o_ref[...] = _dots_casting(a_ref[...], (wd_ref,), (sd_ref,))[0]

    _tile_rows(tr_ref[i], jnp.logical_not(first), o_ref,
               lambda sl: jnp.dot(a_ref[sl, :], sd_ref[...], preferred_element_type=F32))


def _moe_down(tiles, act, wd, tm, tn=512):
    p, f = act.shape
    n = wd.shape[2]
    grid_spec = pltpu.PrefetchScalarGridSpec(
        num_scalar_prefetch=3,
        grid=(n // tn, p // tm),
        in_specs=[pl.BlockSpec((tm, f), lambda j, i, te, tv, tn_: (i, 0)),
                  pl.BlockSpec((None, f, tn), lambda j, i, te, tv, tn_: (te[i], 0, j))],
        out_specs=pl.BlockSpec((tm, tn), lambda j, i, te, tv, tn_: (i, j)),
        scratch_shapes=[pltpu.VMEM((f, tn), BF16)],
    )
    return pl.pallas_call(
        _moe_down_kernel,
        grid_spec=grid_spec,
        out_shape=jax.ShapeDtypeStruct((p, n), F32),
        compiler_params=_cparams(("arbitrary", "arbitrary")),
        name="moe_down",
    )(*tiles, act, wd)


def _moe_combine_kernel(x_ref, ya_ref, yb_ref, route_ref, gate_ref, *rest, final):
    route = route_ref[...]
    f = route[:, 2:3] * ya_ref[...] + route[:, 3:4] * yb_ref[...]
    x = x_ref[...] + gate_ref[...] * f
    if final:
        g_ref, o_ref = rest
        x = x * lax.rsqrt(jnp.mean(x * x, axis=-1, keepdims=True) + RMS_EPS) * g_ref[...]
    else:
        o_ref, = rest
    o_ref[...] = x


def _moe_combine(x2, yab, route, mod3, seq, gate_idx, final_gain, tm=512):
    m, d = x2.shape
    final = final_gain is not None
    in_specs = [pl.BlockSpec((tm, d), lambda i: (i, 0)),
                pl.BlockSpec((tm, d), lambda i: (i, 0)),
                pl.BlockSpec((tm, d), lambda i: (i + m // tm, 0)),
                pl.BlockSpec((tm, LANES), lambda i: (i, 0)),
                pl.BlockSpec((None, 1, d), lambda i: ((i * tm) // seq * 6 + gate_idx, 0, 0))]
    args = [x2, yab, yab, route, mod3]
    if final:
        in_specs.append(pl.BlockSpec((1, d), lambda i: (0, 0)))
        args.append(final_gain.reshape(1, d))
    return pl.pallas_call(
        functools.partial(_moe_combine_kernel, final=final),
        grid=(m // tm,),
        in_specs=in_specs,
        out_specs=pl.BlockSpec((tm, d), lambda i: (i, 0)),
        out_shape=jax.ShapeDtypeStruct((m, d), F32),
        compiler_params=_cparams(("arbitrary",)),
        name="moe_combine",
    )(*args)


def _moe(x2, gain, mod3, router, wg, wu, wd, seq, final_gain, tm=512):
    m, d = x2.shape
    n_experts = router.shape[1]
    router_pad = jnp.pad(router, ((0, 0), (0, LANES - n_experts)))
    h, route = _router(x2, gain, mod3, router_pad, seq, 3, 4, n_experts)

    slots = m * TOP_K
    n_tiles = slots // tm + n_experts
    flat_e = route[:, 0:TOP_K].astype(jnp.int32).reshape(slots)
    onehot = (flat_e[:, None] == jnp.arange(n_experts)[None, :]).astype(jnp.int32)
    rank = jnp.sum((jnp.cumsum(onehot, axis=0) - onehot) * onehot, axis=1)
    counts = jnp.sum(onehot, axis=0)
    padded = (counts + tm - 1) // tm * tm
    ends = jnp.cumsum(padded)
    dest = jnp.sum(onehot * (ends - padded)[None, :], axis=1) + rank
    sorted_tok = (jnp.arange(n_tiles * tm, dtype=jnp.int32) % m).at[dest].set(
        jnp.arange(slots, dtype=jnp.int32) // TOP_K)
    tile_start = jnp.arange(n_tiles, dtype=jnp.int32) * tm
    tile_expert = jnp.minimum(jnp.sum((tile_start[:, None] >= ends[None, :]).astype(jnp.int32), axis=1),
                              n_experts - 1)
    tile_rows = jnp.clip((ends - padded + counts)[tile_expert] - tile_start, 0, tm).astype(jnp.int32)
    tile_new = jnp.concatenate([jnp.ones((1,), jnp.int32),
                                (tile_expert[1:] != tile_expert[:-1]).astype(jnp.int32)])
    tiles = (tile_expert, tile_rows, tile_new)

    xs = _to_bf16(h.at[sorted_tok].get(mode="promise_in_bounds"))
    act = _moe_up(tiles, xs, wg, wu, tm)
    ys = _moe_down(tiles, act, wd, tm)
    dest_slot_major = dest.reshape(m, TOP_K).T.reshape(slots)
    yab = ys.at[dest_slot_major].get(mode="promise_in_bounds")
    return _moe_combine(x2, yab, route, mod3, seq, 5, final_gain)


def _final_norm_kernel(x_ref, g_ref, o_ref):
    x = x_ref[...]
    o_ref[...] = x * lax.rsqrt(jnp.mean(x * x, axis=-1, keepdims=True) + RMS_EPS) * g_ref[...]


def _final_norm(x2, gain, tm=512):
    m, d = x2.shape
    return pl.pallas_call(
        _final_norm_kernel,
        grid=(m // tm,),
        in_specs=[pl.BlockSpec((tm, d), lambda i: (i, 0)),
                  pl.BlockSpec((1, d), lambda i: (0, 0))],
        out_specs=pl.BlockSpec((tm, d), lambda i: (i, 0)),
        out_shape=jax.ShapeDtypeStruct((m, d), F32),
        compiler_params=_cparams(("arbitrary",)),
        name="final_norm",
    )(x2, gain.reshape(1, d))


def _prep_mixer(lp, C):
    w_in, mu = lp["w_in"], lp["shift_mu"]
    n_r = mu.shape[0]
    assert 3 * C < n_r <= 3 * C + LORA_WINDOW <= RWKV_PROJ_COLS
    out = dict(lp)
    out["w_in_r"] = w_in[:, :RWKV_PROJ_COLS].astype(BF16)
    out["w_in_m"] = w_in[:, n_r:].astype(BF16)
    out["mu_pad"] = jnp.pad(mu, (0, 3 * C + LORA_WINDOW - n_r)).reshape(1, -1)
    off = 0
    for name in ("w2", "a2", "g2") + (("v2",) if "v2" in lp else ()):
        rank = lp[name].shape[0]
        out[name + "_pad"] = jnp.pad(lp[name], ((off, LORA_WINDOW - off - rank), (0, 0)))
        off += rank
    assert 3 * C + off == n_r
    return out


def _layer(x2, mod3, lp, batch, seq, v_first, final_gain):
    C = lp["w0"].shape[0]
    lp = _prep_mixer(lp, C)
    h = _normmod(x2, lp["norm_mix"], mod3, seq, 0, 1)
    y_r, v_first = _rwkv(_in_proj(h, lp["w_in_r"]), batch, seq, lp, v_first)
    y_m = _moba(_in_proj(h, lp["w_in_m"]), batch, seq, 0, lp["moba_gain"])
    x2 = _mm_res([y_r, y_m], lp["w_out"], x2, mod3, seq, 2, tm=1024, tn=1024)
    if "ffn_gate" in lp:
        act = _ffn_up(_normmod(x2, lp["norm_ffn"], mod3, seq, 3, 4), lp["ffn_gate"], lp["ffn_up"])
        x2 = _mm_res([act], lp["ffn_down"], x2, mod3, seq, 5)
        if final_gain is not None:
            x2 = _final_norm(x2, final_gain)
    else:
        x2 = _moe(x2, lp["norm_ffn"], mod3, lp["router"], lp["exp_gate"], lp["exp_up"],
                  lp["exp_down"], seq, final_gain)
    return x2, v_first


def _forward(x, c, layers, norm_out):
    batch, seq, d = x.shape
    x2 = x.reshape(batch * seq, d)
    c_pad = jnp.pad(c, ((0, SUBLANES - batch), (0, 0)))
    v_first = None
    for li, lp in enumerate(layers):
        mod = _adaln(c_pad, lp["mod_w"], lp["mod_b"])[:batch]
        mod3 = mod.reshape(batch * 6, 1, d)
        last = li == len(layers) - 1
        x2, v_first = _layer(x2, mod3, lp, batch, seq, v_first, norm_out if last else None)
    return x2.reshape(batch, seq, d)


def kernel(x, c, l0_mod_w, l0_mod_b, l0_norm_mix, l0_w_in, l0_shift_mu, l0_w0, l0_w2, l0_a0, l0_a2, l0_g2, l0_k_k, l0_k_a, l0_r_k, l0_ln_w, l0_ln_b, l0_moba_gain, l0_w_out, l0_norm_ffn, l0_ffn_gate, l0_ffn_up, l0_ffn_down, l1_mod_w, l1_mod_b, l1_norm_mix, l1_w_in, l1_shift_mu, l1_w0, l1_w2, l1_a0, l1_a2, l1_g2, l1_v0, l1_v2, l1_k_k, l1_k_a, l1_r_k, l1_ln_w, l1_ln_b, l1_moba_gain, l1_w_out, l1_norm_ffn, l1_router, l1_exp_gate, l1_exp_up, l1_exp_down, norm_out):
    layers = (
        dict(mod_w=l0_mod_w, mod_b=l0_mod_b, norm_mix=l0_norm_mix, w_in=l0_w_in, shift_mu=l0_shift_mu,
             w0=l0_w0, w2=l0_w2, a0=l0_a0, a2=l0_a2, g2=l0_g2, k_k=l0_k_k, k_a=l0_k_a, r_k=l0_r_k,
             ln_w=l0_ln_w, ln_b=l0_ln_b, moba_gain=l0_moba_gain, w_out=l0_w_out, norm_ffn=l0_norm_ffn,
             ffn_gate=l0_ffn_gate, ffn_up=l0_ffn_up, ffn_down=l0_ffn_down),
        dict(mod_w=l1_mod_w, mod_b=l1_mod_b, norm_mix=l1_norm_mix, w_in=l1_w_in, shift_mu=l1_shift_mu,
             w0=l1_w0, w2=l1_w2, a0=l1_a0, a2=l1_a2, g2=l1_g2, v0=l1_v0, v2=l1_v2, k_k=l1_k_k,
             k_a=l1_k_a, r_k=l1_r_k, ln_w=l1_ln_w, ln_b=l1_ln_b, moba_gain=l1_moba_gain, w_out=l1_w_out,
             norm_ffn=l1_norm_ffn, router=l1_router, exp_gate=l1_exp_gate, exp_up=l1_exp_up,
             exp_down=l1_exp_down),
    )
    return _forward(x, c, layers, norm_out)
```

```python
import functools

import jax
import jax.numpy as jnp
from jax import lax
from jax.experimental import pallas as pl
from jax.experimental.pallas import tpu as pltpu

F32 = jnp.float32
BF16 = jnp.bfloat16

RMS_EPS = 1e-6
GN_EPS = 64e-5
RWKV_HEAD_DIM = 64
MOBA_HEAD_DIM = 128
MOBA_BLOCK = 256
MOBA_TOPK = 3
TOP_K = 2
LANES = 128
SUBLANES = 8
RWKV_CHUNK = 64
VMEM_LIMIT = 52 * 1024 * 1024
CAST_ROWS = 256
MOE_CAST_SLAB = 512
LOG2E = 1.4426950408889634

LORA_WINDOW = 384
RWKV_PROJ_COLS = 3584


def _cparams(sem):
    return pltpu.CompilerParams(dimension_semantics=sem, vmem_limit_bytes=VMEM_LIMIT)


def _bdot(a, b):
    return jnp.dot(a.astype(BF16), b.astype(BF16), preferred_element_type=F32)


def _bdot_nt(a, b):
    return lax.dot_general(a.astype(BF16), b.astype(BF16), (((1,), (1,)), ((), ())),
                           preferred_element_type=F32)


def _bdot_tn(a, b):
    return lax.dot_general(a.astype(BF16), b.astype(BF16), (((0,), (0,)), ((), ())),
                           preferred_element_type=F32)


def _split(x, terms):
    parts = []
    rem = x
    for _ in range(terms):
        hi = rem.astype(BF16)
        parts.append(hi)
        rem = rem - hi.astype(F32)
    return parts


def _dot_xe(x, e_bf16, terms):
    acc = None
    for part in _split(x, terms):
        d = jnp.dot(part, e_bf16, preferred_element_type=F32)
        acc = d if acc is None else acc + d
    return acc


def _dot_ex(e_bf16, x, terms):
    acc = None
    for part in _split(x, terms):
        d = jnp.dot(e_bf16, part, preferred_element_type=F32)
        acc = d if acc is None else acc + d
    return acc


def _dot3(a, b, nt=False):
    a_hi, a_lo = _split(a, 2)
    b_hi, b_lo = _split(b, 2)
    if nt:
        f = lambda u, v: lax.dot_general(u, v, (((1,), (1,)), ((), ())), preferred_element_type=F32)
    else:
        f = lambda u, v: jnp.dot(u, v, preferred_element_type=F32)
    return f(a_hi, b_hi) + (f(a_lo, b_hi) + f(a_hi, b_lo))


def _norm_mod(x, gain, shift, scale):
    y = x * lax.rsqrt(jnp.mean(x * x, axis=-1, keepdims=True) + RMS_EPS)
    return (y * gain) * (1.0 + scale) + shift


def _cast_block(w_ref, s_ref):
    if w_ref.dtype == BF16:
        return
    rows = w_ref.shape[0]
    step = CAST_ROWS if rows % CAST_ROWS == 0 else rows

    def body(r, carry):
        sl = pl.ds(pl.multiple_of(r * step, step), step)
        s_ref[sl, :] = w_ref[sl, :].astype(BF16)
        return carry

    lax.fori_loop(0, rows // step, body, 0)


def _mod_kernel(c_ref, w_ref, b_ref, o_ref):
    c = c_ref[...]
    o_ref[...] = _dot3(c * jax.nn.sigmoid(c), w_ref[...]) + b_ref[...]


def _adaln(c_pad, mod_w, mod_b, tn=512):
    m, d = c_pad.shape
    n = mod_w.shape[1]
    return pl.pallas_call(
        _mod_kernel,
        grid=(n // tn,),
        in_specs=[pl.BlockSpec((m, d), lambda j: (0, 0)),
                  pl.BlockSpec((d, tn), lambda j: (0, j)),
                  pl.BlockSpec((1, tn), lambda j: (0, j))],
        out_specs=pl.BlockSpec((m, tn), lambda j: (0, j)),
        out_shape=jax.ShapeDtypeStruct((m, n), F32),
        compiler_params=_cparams(("arbitrary",)),
        name="adaln",
    )(c_pad, mod_w, mod_b.reshape(1, n))


def _normmod_kernel(x_ref, g_ref, sh_ref, sc_ref, o_ref):
    o_ref[...] = _norm_mod(x_ref[...], g_ref[...], sh_ref[...], sc_ref[...]).astype(BF16)


def _normmod(x2, gain, mod3, seq, shift_idx, scale_idx, tm=512):
    m, d = x2.shape
    return pl.pallas_call(
        _normmod_kernel,
        grid=(m // tm,),
        in_specs=[pl.BlockSpec((tm, d), lambda i: (i, 0)),
                  pl.BlockSpec((1, d), lambda i: (0, 0)),
                  pl.BlockSpec((None, 1, d), lambda i: ((i * tm) // seq * 6 + shift_idx, 0, 0)),
                  pl.BlockSpec((None, 1, d), lambda i: ((i * tm) // seq * 6 + scale_idx, 0, 0))],
        out_specs=pl.BlockSpec((tm, d), lambda i: (i, 0)),
        out_shape=jax.ShapeDtypeStruct((m, d), BF16),
        compiler_params=_cparams(("arbitrary",)),
        name="normmod",
    )(x2, gain.reshape(1, d), mod3, mod3)


def _in_proj_kernel(h_ref, w_ref, o_ref):
    o_ref[...] = jnp.dot(h_ref[...], w_ref[...], preferred_element_type=F32)


def _in_proj(h, w, tm=1024):
    m, d = h.shape
    n = w.shape[1]
    tm = min(tm, m)
    tn = n // 2
    assert tn % (2 * LANES) == 0
    return pl.pallas_call(
        _in_proj_kernel,
        grid=(n // tn, m // tm),
        in_specs=[pl.BlockSpec((tm, d), lambda j, i: (i, 0)),
                  pl.BlockSpec((d, tn), lambda j, i: (0, j))],
        out_specs=pl.BlockSpec((tm, tn), lambda j, i: (i, j)),
        out_shape=jax.ShapeDtypeStruct((m, n), F32),
        compiler_params=_cparams(("arbitrary", "arbitrary")),
        name="in_proj",
    )(h, w)


def _ffn_up_kernel(h_ref, wg_ref, wu_ref, o_ref, sg_ref, su_ref):
    @pl.when(pl.program_id(1) == 0)
    def _():
        _cast_block(wg_ref, sg_ref)
        _cast_block(wu_ref, su_ref)

    h = h_ref[...]
    gate = jnp.dot(h, sg_ref[...], preferred_element_type=F32)
    up = jnp.dot(h, su_ref[...], preferred_element_type=F32)
    o_ref[...] = (jax.nn.silu(gate) * up).astype(BF16)


def _ffn_up(h, wg, wu, tm=1024, tn=512):
    m, d = h.shape
    tm = min(tm, m)
    n = wg.shape[1]
    return pl.pallas_call(
        _ffn_up_kernel,
        grid=(n // tn, m // tm),
        in_specs=[pl.BlockSpec((tm, d), lambda j, i: (i, 0)),
                  pl.BlockSpec((d, tn), lambda j, i: (0, j)),
                  pl.BlockSpec((d, tn), lambda j, i: (0, j))],
        out_specs=pl.BlockSpec((tm, tn), lambda j, i: (i, j)),
        out_shape=jax.ShapeDtypeStruct((m, n), BF16),
        scratch_shapes=[pltpu.VMEM((d, tn), BF16), pltpu.VMEM((d, tn), BF16)],
        compiler_params=_cparams(("arbitrary", "arbitrary")),
        name="ffn_up",
    )(h, wg, wu)


def _mm_res_kernel(*refs, n_lhs):
    lhs = refs[:n_lhs]
    ws = refs[n_lhs:2 * n_lhs]
    res_ref, gate_ref, o_ref = refs[2 * n_lhs:2 * n_lhs + 3]
    scr = refs[2 * n_lhs + 3:]

    @pl.when(pl.program_id(1) == 0)
    def _():
        for w_ref, s_ref in zip(ws, scr):
            _cast_block(w_ref, s_ref)

    acc = None
    for l_ref, s_ref in zip(lhs, scr):
        d = jnp.dot(l_ref[...], s_ref[...], preferred_element_type=F32)
        acc = d if acc is None else acc + d
    o_ref[...] = res_ref[...] + gate_ref[...] * acc


def _mm_res(lhs_list, w, res, mod3, seq, gate_idx, tm=512, tn=512):
    m, n = res.shape
    tm = min(tm, m)
    k = lhs_list[0].shape[1]
    n_lhs = len(lhs_list)
    in_specs = [pl.BlockSpec((tm, k), lambda j, i: (i, 0)) for _ in lhs_list]
    in_specs += [pl.BlockSpec((k, tn), functools.partial(lambda j, i, li: (li, j), li=li))
                 for li in range(n_lhs)]
    in_specs += [pl.BlockSpec((tm, tn), lambda j, i: (i, j)),
                 pl.BlockSpec((None, 1, tn), lambda j, i: ((i * tm) // seq * 6 + gate_idx, 0, j))]
    return pl.pallas_call(
        functools.partial(_mm_res_kernel, n_lhs=n_lhs),
        grid=(n // tn, m // tm),
        in_specs=in_specs,
        out_specs=pl.BlockSpec((tm, tn), lambda j, i: (i, j)),
        out_shape=jax.ShapeDtypeStruct((m, n), F32),
        scratch_shapes=[pltpu.VMEM((k, tn), BF16) for _ in lhs_list],
        compiler_params=_cparams(("arbitrary", "arbitrary")),
        name="mm_res",
    )(*lhs_list, *([w] * n_lhs), res, mod3)


def _rwkv_kernel(*refs, chunk, width, vres):
    it = iter(refs)
    p_ref, mu_ref, w0_ref, w2_ref, a0_ref, a2_ref, g2_ref = (next(it) for _ in range(7))
    if vres:
        v0_ref, v2_ref, vf_ref = next(it), next(it), next(it)
    kk_ref, ka_ref, rk_ref, lnw_ref, lnb_ref = (next(it) for _ in range(5))
    o_ref = next(it)
    vf_out_ref = None if vres else next(it)
    carry_scr, state_scr = next(it), next(it)

    L, C = chunk, width
    P2 = 2 * L
    pairs = list(range(C // LANES))
    ci = pl.program_id(1)

    @pl.when(ci == 0)
    def _():
        carry_scr[...] = jnp.zeros_like(carry_scr)
        state_scr[...] = jnp.zeros_like(state_scr)

    def shifted(lo, hi):
        p = p_ref[:, lo:hi]
        rolled = pltpu.roll(p, 1, 0)
        row = lax.broadcasted_iota(jnp.int32, p.shape, 0)
        prev = jnp.where(row == 0, carry_scr[SUBLANES - 1:SUBLANES, lo:hi], rolled)
        return p + (prev - p) * mu_ref[:, lo:hi]

    zl = shifted(3 * C, 3 * C + LORA_WINDOW)
    wl = w0_ref[...] + _bdot(jnp.tanh(zl), w2_ref[...])
    logdecay = -jnp.exp(-jax.nn.softplus(-wl) - 0.5)
    a_all = jax.nn.sigmoid(a0_ref[...] + _bdot(zl, a2_ref[...]))
    g_all = _bdot(jax.nn.sigmoid(zl), g2_ref[...])
    if vres:
        vmix = jax.nn.sigmoid(v0_ref[...] + _bdot(zl, v2_ref[...]))

    tr = lax.broadcasted_iota(jnp.int32, (L, L), 0)
    tc = lax.broadcasted_iota(jnp.int32, (L, L), 1)
    tri = (tc <= tr).astype(BF16)
    cum = _dot_ex(tri, logdecay, 3)
    p_inc = jnp.exp(cum)
    p_exc = jnp.exp(cum - logdecay)
    p_inv = jnp.exp(-cum)

    rr = lax.broadcasted_iota(jnp.int32, (P2, P2), 0)
    cc = lax.broadcasted_iota(jnp.int32, (P2, P2), 1)
    strict = cc < rr
    incl = cc <= rr
    eye = (cc == rr).astype(F32)
    lane = lax.broadcasted_iota(jnp.int32, (L, LANES), 1)
    head0 = lane < RWKV_HEAD_DIM
    seg = ((rr // RWKV_HEAD_DIM) == (cc // RWKV_HEAD_DIM)).astype(BF16)

    def expand(a):
        return jnp.concatenate([jnp.where(head0, a, 0.0), jnp.where(head0, 0.0, a)], axis=0)

    sls = [slice(hp * LANES, (hp + 1) * LANES) for hp in pairs]
    def seg_sums(xs):
        tot = _dot_xe(jnp.concatenate(xs, axis=0), seg, 2)
        return [tot[i * L:(i + 1) * L] for i in range(len(xs))]

    r_l, k_l, v_l, kk0_l = [], [], [], []
    for hp in pairs:
        sl = sls[hp]
        r_l.append(shifted(hp * LANES, (hp + 1) * LANES))
        k_l.append(shifted(C + hp * LANES, C + (hp + 1) * LANES))
        v = shifted(2 * C + hp * LANES, 2 * C + (hp + 1) * LANES)
        if vres:
            v = v + (vf_ref[:, sl] - v) * vmix[:, sl]
        else:
            vf_out_ref[:, sl] = v
        v_l.append(v)
        kk0 = k_l[hp] * kk_ref[:, sl]
        kk0_l.append(kk0)
    ssq_l = seg_sums([kk0 * kk0 for kk0 in kk0_l])

    lhs_l, rhs_l, ax_l, vx_l, k2_l = [], [], [], [], []
    for hp in pairs:
        sl = sls[hp]
        a = a_all[:, sl]
        kk = kk0_l[hp] * lax.rsqrt(jnp.maximum(ssq_l[hp], 1e-24))
        k2 = k_l[hp] * (1.0 + (a - 1.0) * ka_ref[:, sl])
        k2_l.append(k2)
        ax = expand(-kk * p_exc[:, sl]).astype(BF16)
        rx = expand(r_l[hp] * p_inc[:, sl]).astype(BF16)
        bx = expand(kk * a * p_inv[:, sl]).astype(BF16)
        kx = expand(k2 * p_inv[:, sl]).astype(BF16)
        ax_l.append(ax)
        vx_l.append(expand(v_l[hp]).astype(BF16))
        lhs_l.append(jnp.concatenate([ax, rx], axis=0))
        rhs_l.append(jnp.concatenate([bx, kx], axis=0))

    gram_l = [_bdot_nt(lhs_l[hp], rhs_l[hp]) for hp in pairs]
    a_ak_l = [jnp.where(strict, g[:P2, P2:], 0.0).astype(BF16) for g in gram_l]
    a_r_l = [jnp.concatenate([jnp.where(incl, g[P2:, :P2], 0.0),
                              jnp.where(incl, g[P2:, P2:], 0.0)], axis=1).astype(BF16) for g in gram_l]

    n_l = [jnp.where(strict, g[:P2, :P2], 0.0) for g in gram_l]
    tinv_l = [eye + n for n in n_l]
    n_l = [_bdot(n, n) for n in n_l]
    span = 4
    while span < L:
        prod_l = [_bdot(jnp.concatenate([t, n], axis=0), n) for t, n in zip(tinv_l, n_l)]
        tinv_l = [t + p[:P2] for t, p in zip(tinv_l, prod_l)]
        n_l = [p[P2:] for p in prod_l]
        span *= 2
    tinv_l = [t + _bdot(t, n) for t, n in zip(tinv_l, n_l)]

    akv_l = [_bdot(a_ak_l[hp], vx_l[hp]) for hp in pairs]
    w_l = [_bdot(tinv_l[hp], jnp.concatenate([ax_l[hp], akv_l[hp].astype(BF16)], axis=1)) for hp in pairs]

    s0_l = [state_scr[hp] for hp in pairs]
    z_l = [_bdot_nt(jnp.concatenate([w_l[hp][:, :LANES].astype(BF16), lhs_l[hp][P2:]], axis=0), s0_l[hp])
           for hp in pairs]
    uv_l = [jnp.concatenate([(z_l[hp][:P2] + w_l[hp][:, LANES:]).astype(BF16), vx_l[hp]], axis=0)
            for hp in pairs]
    y2_l = [z_l[hp][P2:] + _bdot(a_r_l[hp], uv_l[hp]) for hp in pairs]
    for hp in pairs:
        s_new = s0_l[hp] + _bdot_tn(uv_l[hp], rhs_l[hp])
        state_scr[hp] = s_new * p_inc[L - 1:L, sls[hp]]

    y_l = [y2[:L] + y2[L:] for y2 in y2_l]
    mean_l = [t * (1.0 / RWKV_HEAD_DIM) for t in seg_sums(y_l)]
    dev_l = [y - mu for y, mu in zip(y_l, mean_l)]
    var_l = [t * (1.0 / RWKV_HEAD_DIM) for t in seg_sums([d * d for d in dev_l])]
    bon_l = seg_sums([r_l[hp] * k2_l[hp] * rk_ref[:, sls[hp]] for hp in pairs])
    for hp in pairs:
        sl = sls[hp]
        yn = dev_l[hp] * lax.rsqrt(var_l[hp] + GN_EPS) * lnw_ref[:, sl] + lnb_ref[:, sl]
        o_ref[:, sl] = ((yn + bon_l[hp] * v_l[hp]) * g_all[:, sl]).astype(o_ref.dtype)

    carry_scr[...] = p_ref[L - SUBLANES:L, :]


def _rwkv(proj, batch, seq, lp, v_first):
    vres = v_first is not None
    C = lp["w0"].shape[0]
    L = RWKV_CHUNK
    W = 3 * C + LORA_WINDOW
    row = lambda a: a.reshape(1, -1)
    full = lambda a: pl.BlockSpec(a.shape, lambda b, c: (0, 0))
    tok = pl.BlockSpec((L, C), lambda b, c: (b * (seq // L) + c, 0))
    args = [proj, lp["mu_pad"], row(lp["w0"]), lp["w2_pad"], row(lp["a0"]), lp["a2_pad"], lp["g2_pad"]]
    in_specs = [pl.BlockSpec((L, W), lambda b, c: (b * (seq // L) + c, 0))]
    in_specs += [full(a) for a in args[1:]]
    if vres:
        extra = [row(lp["v0"]), lp["v2_pad"]]
        args += extra + [v_first]
        in_specs += [full(a) for a in extra] + [tok]
    tail = [row(lp["k_k"]), row(lp["k_a"]), row(lp["r_k"]), row(lp["ln_w"]), row(lp["ln_b"])]
    args += tail
    in_specs += [full(a) for a in tail]
    y_sd = jax.ShapeDtypeStruct((batch * seq, C), BF16)
    v_sd = jax.ShapeDtypeStruct((batch * seq, C), F32)
    out = pl.pallas_call(
        functools.partial(_rwkv_kernel, chunk=L, width=C, vres=vres),
        grid=(batch, seq // L),
        in_specs=in_specs,
        out_specs=tok if vres else (tok, tok),
        out_shape=y_sd if vres else (y_sd, v_sd),
        scratch_shapes=[pltpu.VMEM((SUBLANES, W), F32),
                        pltpu.VMEM((C // LANES, LANES, LANES), F32)],
        compiler_params=_cparams(("arbitrary", "arbitrary")),
        name="rwkv7",
    )(*args)
    return (out, v_first) if vres else out


def _moba_kernel(q_ref, k_ref, v_ref, slope_ref, gain_ref, o_ref, *, n_blocks):
    BS = MOBA_BLOCK
    T = n_blocks * BS
    scale = MOBA_HEAD_DIM ** -0.5

    means = [jnp.mean(k_ref[j * BS:(j + 1) * BS, :], axis=0, keepdims=True) for j in range(n_blocks)]
    km = jnp.concatenate(means + [jnp.zeros((LANES - n_blocks, LANES), F32)], axis=0)
    gate_t = _dot3(km, q_ref[...], nt=True)[0:SUBLANES, :]
    blk = lax.broadcasted_iota(jnp.int32, (SUBLANES, T), 0)
    own_of = lax.broadcasted_iota(jnp.int32, (SUBLANES, T), 1) // BS
    past = blk < own_of
    gate_t = jnp.where(past, gate_t, -jnp.inf)
    beaten = jnp.zeros((SUBLANES, T), jnp.int32)
    for j in range(n_blocks):
        gj = gate_t[j:j + 1, :]
        ahead = (gj > gate_t) | ((gj == gate_t) & (blk > j))
        beaten = beaten + ahead.astype(jnp.int32)
    sel_t = ((beaten < MOBA_TOPK) & past).astype(F32)

    ind = (lax.broadcasted_iota(jnp.int32, (LANES, T), 1) // BS
           == lax.broadcasted_iota(jnp.int32, (LANES, T), 0)).astype(BF16)
    slope = slope_ref[:, 0:1] * LOG2E
    bias = slope * lax.broadcasted_iota(jnp.int32, (1, T), 1).astype(F32)
    causal = (lax.broadcasted_iota(jnp.int32, (BS, BS), 1)
              <= lax.broadcasted_iota(jnp.int32, (BS, BS), 0))
    zpad = jnp.zeros((LANES - SUBLANES, BS), F32)

    for own in range(n_blocks):
        lo, hi = own * BS, (own + 1) * BS
        q = (q_ref[lo:hi, :] * (scale * LOG2E)).astype(BF16)
        s_o = _bdot_nt(q, k_ref[lo:hi, :]) + bias[:, 0:BS]
        s_o = jnp.where(causal, s_o, -jnp.inf)
        m = jnp.max(s_o, axis=-1, keepdims=True)
        if own > 0:
            s_p = _bdot_nt(q, k_ref[0:lo, :]) + (bias[:, 0:lo] - slope * float(lo))
            if own > MOBA_TOPK:
                sel = jnp.concatenate([sel_t[:, lo:hi], zpad], axis=0).T.astype(BF16)
                picked = jnp.dot(sel, ind[:, 0:lo], preferred_element_type=F32) > 0.5
                s_p = jnp.where(picked, s_p, -jnp.inf)
            m = jnp.maximum(m, jnp.max(s_p, axis=-1, keepdims=True))
            e_p = jnp.exp2(s_p - m)
        e_o = jnp.exp2(s_o - m)
        denom = jnp.sum(e_o, axis=-1, keepdims=True)
        out = _bdot(e_o, v_ref[lo:hi, :])
        if own > 0:
            denom = denom + jnp.sum(e_p, axis=-1, keepdims=True)
            out = out + _bdot(e_p, v_ref[0:lo, :])
        out = out / denom
        out = out * lax.rsqrt(jnp.mean(out * out, axis=-1, keepdims=True) + RMS_EPS)
        o_ref[lo:hi, :] = (out * gain_ref[...]).astype(o_ref.dtype)


def _moba(proj, batch, seq, q_off, gain):
    width = gain.shape[0]
    heads = width // MOBA_HEAD_DIM
    nb = seq // MOBA_BLOCK
    assert nb <= SUBLANES and seq % MOBA_BLOCK == 0
    qo = q_off // LANES
    slopes = jnp.exp2(-8.0 * jnp.arange(1, heads + 1, dtype=F32) / heads)
    slopes = jnp.broadcast_to(slopes[:, None, None], (heads, 1, LANES))
    return pl.pallas_call(
        functools.partial(_moba_kernel, n_blocks=nb),
        grid=(batch, heads),
        in_specs=[pl.BlockSpec((seq, LANES), lambda b, h: (b, qo + h)),
                  pl.BlockSpec((seq, LANES), lambda b, h: (b, qo + heads + h)),
                  pl.BlockSpec((seq, LANES), lambda b, h: (b, qo + 2 * heads + h)),
                  pl.BlockSpec((None, 1, LANES), lambda b, h: (h, 0, 0)),
                  pl.BlockSpec((1, LANES), lambda b, h: (0, h))],
        out_specs=pl.BlockSpec((seq, LANES), lambda b, h: (b, h)),
        out_shape=jax.ShapeDtypeStruct((batch * seq, width), BF16),
        compiler_params=_cparams(("arbitrary", "arbitrary")),
        name="moba",
    )(proj, proj, proj, slopes, gain.reshape(1, width))


def _router_kernel(x_ref, g_ref, sh_ref, sc_ref, wr_ref, h_ref, route_ref, *, n_experts):
    h = _norm_mod(x_ref[...], g_ref[...], sh_ref[...], sc_ref[...])
    h_ref[...] = h
    logits = _dot3(h, wr_ref[...])
    col = lax.broadcasted_iota(jnp.int32, logits.shape, 1)
    neg = -jnp.inf
    logits = jnp.where(col < n_experts, logits, neg)
    m1 = jnp.max(logits, axis=-1, keepdims=True)
    i1 = jnp.min(jnp.where(logits == m1, col, LANES), axis=-1, keepdims=True)
    rest = jnp.where(col == i1, neg, logits)
    m2 = jnp.max(rest, axis=-1, keepdims=True)
    i2 = jnp.min(jnp.where(rest == m2, col, LANES), axis=-1, keepdims=True)
    e2 = jnp.exp(m2 - m1)
    wa = 1.0 / (1.0 + e2)
    wb = e2 / (1.0 + e2)
    route = jnp.where(col == 0, i1.astype(F32), 0.0)
    route = jnp.where(col == 1, i2.astype(F32), route)
    route = jnp.where(col == 2, wa, route)
    route = jnp.where(col == 3, wb, route)
    route_ref[...] = route


def _router(x2, gain, mod3, router_pad, seq, shift_idx, scale_idx, n_experts, tm=512):
    m, d = x2.shape
    sh = pl.BlockSpec((None, 1, d), lambda i: ((i * tm) // seq * 6 + shift_idx, 0, 0))
    sc = pl.BlockSpec((None, 1, d), lambda i: ((i * tm) // seq * 6 + scale_idx, 0, 0))
    return pl.pallas_call(
        functools.partial(_router_kernel, n_experts=n_experts),
        grid=(m // tm,),
        in_specs=[pl.BlockSpec((tm, d), lambda i: (i, 0)),
                  pl.BlockSpec((1, d), lambda i: (0, 0)),
                  sh, sc,
                  pl.BlockSpec((d, LANES), lambda i: (0, 0))],
        out_specs=(pl.BlockSpec((tm, d), lambda i: (i, 0)),
                   pl.BlockSpec((tm, LANES), lambda i: (i, 0))),
        out_shape=(jax.ShapeDtypeStruct((m, d), F32),
                   jax.ShapeDtypeStruct((m, LANES), F32)),
        compiler_params=_cparams(("arbitrary",)),
        name="router",
    )(x2, gain.reshape(1, d), mod3, mod3, router_pad)


def _to_bf16_kernel(x_ref, o_ref):
    o_ref[...] = x_ref[...].astype(BF16)


def _to_bf16(x, tm=1024):
    m, d = x.shape
    tm = min(tm, m)
    return pl.pallas_call(
        _to_bf16_kernel,
        grid=(m // tm,),
        in_specs=[pl.BlockSpec((tm, d), lambda i: (i, 0))],
        out_specs=pl.BlockSpec((tm, d), lambda i: (i, 0)),
        out_shape=jax.ShapeDtypeStruct((m, d), BF16),
        compiler_params=_cparams(("arbitrary",)),
        name="to_bf16",
    )(x)


def _tile_rows(rows, ok, o_ref, compute):
    tm = o_ref.shape[0]
    half = tm // 2

    @pl.when(ok & (rows > half))
    def _():
        o_ref[...] = compute(slice(0, tm))

    @pl.when(ok & (rows > 0) & (rows <= half))
    def _():
        o_ref[0:half, :] = compute(slice(0, half))
        o_ref[half:tm, :] = jnp.zeros((tm - half, o_ref.shape[1]), o_ref.dtype)

    @pl.when(ok & (rows == 0))
    def _():
        o_ref[...] = jnp.zeros_like(o_ref)


def _dots_casting(x, w_refs, s_refs):
    accs = [None] * len(w_refs)
    for k0 in range(0, x.shape[1], MOE_CAST_SLAB):
        ks = slice(k0, k0 + MOE_CAST_SLAB)
        for n, (w_ref, s_ref) in enumerate(zip(w_refs, s_refs)):
            wk = w_ref[ks, :].astype(BF16)
            s_ref[ks, :] = wk
            d = jnp.dot(x[:, ks], wk, preferred_element_type=F32)
            accs[n] = d if accs[n] is None else accs[n] + d
    return accs


def _fetch_ahead(w_hbm, bufs, sems, sched, j, i, cols):
    te_ref, _, _, tg_ref, nx_ref, ng_ref = sched
    n_j = pl.num_programs(0)
    ng = ng_ref[0]
    g = tg_ref[i]
    seg = j * ng + g
    slot = seg % 2

    def copies(expert, jj, s):
        col = pl.multiple_of(jj * cols, cols)
        return [pltpu.make_async_copy(w.at[expert, :, pl.ds(col, cols)], b.at[s], sems.at[n, s])
                for n, (w, b) in enumerate(zip(w_hbm, bufs))]

    @pl.when(seg == 0)
    def _():
        for c in copies(te_ref[i], j, slot):
            c.start()

    for c in copies(te_ref[i], j, slot):
        c.wait()

    @pl.when(jnp.logical_not((j == n_j - 1) & (g == ng - 1)))
    def _():
        jn = jnp.where(g == ng - 1, j + 1, j)
        for c in copies(nx_ref[i], jn, 1 - slot):
            c.start()

    return slot


def _moe_up_kernel(*refs):
    sched, (x_ref, wg_hbm, wu_hbm, o_ref, sg_ref, su_ref, bg_ref, bu_ref, sems) = refs[:6], refs[6:]
    j, i = pl.program_id(0), pl.program_id(1)
    first = sched[2][i] > 0

    @pl.when(first)
    def _():
        slot = _fetch_ahead((wg_hbm, wu_hbm), (bg_ref, bu_ref), sems, sched, j, i, o_ref.shape[1])
        gate, up = _dots_casting(x_ref[...], (bg_ref.at[slot], bu_ref.at[slot]), (sg_ref, su_ref))
        o_ref[...] = (jax.nn.silu(gate) * up).astype(BF16)

    def compute(sl):
        x = x_ref[sl, :]
        gate = jnp.dot(x, sg_ref[...], preferred_element_type=F32)
        up = jnp.dot(x, su_ref[...], preferred_element_type=F32)
        return (jax.nn.silu(gate) * up).astype(BF16)

    _tile_rows(sched[1][i], jnp.logical_not(first), o_ref, compute)


def _moe_up(sched, xs, wg, wu, tm, tf=512):
    p, d = xs.shape
    f = wg.shape[2]
    grid_spec = pltpu.PrefetchScalarGridSpec(
        num_scalar_prefetch=len(sched),
        grid=(f // tf, p // tm),
        in_specs=[pl.BlockSpec((tm, d), lambda j, i, *_: (i, 0)),
                  pl.BlockSpec(memory_space=pl.ANY),
                  pl.BlockSpec(memory_space=pl.ANY)],
        out_specs=pl.BlockSpec((tm, tf), lambda j, i, *_: (i, j)),
        scratch_shapes=[pltpu.VMEM((d, tf), BF16), pltpu.VMEM((d, tf), BF16),
                        pltpu.VMEM((2, d, tf), F32), pltpu.VMEM((2, d, tf), F32),
                        pltpu.SemaphoreType.DMA((2, 2))],
    )
    return pl.pallas_call(
        _moe_up_kernel,
        grid_spec=grid_spec,
        out_shape=jax.ShapeDtypeStruct((p, f), BF16),
        compiler_params=_cparams(("arbitrary", "arbitrary")),
        name="moe_up",
    )(*sched, xs, wg, wu)


def _moe_down_kernel(*refs):
    sched, (a_ref, wd_hbm, o_ref, sd_ref, bd_ref, sems) = refs[:6], refs[6:]
    j, i = pl.program_id(0), pl.program_id(1)
    first = sched[2][i] > 0

    @pl.when(first)
    def _():
        slot = _fetch_ahead((wd_hbm,), (bd_ref,), sems, sched, j, i, o_ref.shape[1])
        o_ref[...] = _dots_casting(a_ref[...], (bd_ref.at[slot],), (sd_ref,))[0]

    _tile_rows(sched[1][i], jnp.logical_not(first), o_ref,
               lambda sl: jnp.dot(a_ref[sl, :], sd_ref[...], preferred_element_type=F32))


def _moe_down(sched, act, wd, tm, tn=512):
    p, f = act.shape
    n = wd.shape[2]
    grid_spec = pltpu.PrefetchScalarGridSpec(
        num_scalar_prefetch=len(sched),
        grid=(n // tn, p // tm),
        in_specs=[pl.BlockSpec((tm, f), lambda j, i, *_: (i, 0)),
                  pl.BlockSpec(memory_space=pl.ANY)],
        out_specs=pl.BlockSpec((tm, tn), lambda j, i, *_: (i, j)),
        scratch_shapes=[pltpu.VMEM((f, tn), BF16), pltpu.VMEM((2, f, tn), F32),
                        pltpu.SemaphoreType.DMA((1, 2))],
    )
    return pl.pallas_call(
        _moe_down_kernel,
        grid_spec=grid_spec,
        out_shape=jax.ShapeDtypeStruct((p, n), F32),
        compiler_params=_cparams(("arbitrary", "arbitrary")),
        name="moe_down",
    )(*sched, act, wd)


def _moe_combine_kernel(x_ref, ya_ref, yb_ref, route_ref, gate_ref, *rest, final):
    route = route_ref[...]
    f = route[:, 2:3] * ya_ref[...] + route[:, 3:4] * yb_ref[...]
    x = x_ref[...] + gate_ref[...] * f
    if final:
        g_ref, o_ref = rest
        x = x * lax.rsqrt(jnp.mean(x * x, axis=-1, keepdims=True) + RMS_EPS) * g_ref[...]
    else:
        o_ref, = rest
    o_ref[...] = x


def _moe_combine(x2, yab, route, mod3, seq, gate_idx, final_gain, tm=512):
    m, d = x2.shape
    final = final_gain is not None
    in_specs = [pl.BlockSpec((tm, d), lambda i: (i, 0)),
                pl.BlockSpec((tm, d), lambda i: (i, 0)),
                pl.BlockSpec((tm, d), lambda i: (i + m // tm, 0)),
                pl.BlockSpec((tm, LANES), lambda i: (i, 0)),
                pl.BlockSpec((None, 1, d), lambda i: ((i * tm) // seq * 6 + gate_idx, 0, 0))]
    args = [x2, yab, yab, route, mod3]
    if final:
        in_specs.append(pl.BlockSpec((1, d), lambda i: (0, 0)))
        args.append(final_gain.reshape(1, d))
    return pl.pallas_call(
        functools.partial(_moe_combine_kernel, final=final),
        grid=(m // tm,),
        in_specs=in_specs,
        out_specs=pl.BlockSpec((tm, d), lambda i: (i, 0)),
        out_shape=jax.ShapeDtypeStruct((m, d), F32),
        compiler_params=_cparams(("arbitrary",)),
        name="moe_combine",
    )(*args)


def _moe(x2, gain, mod3, router, wg, wu, wd, seq, final_gain, tm=512):
    m, d = x2.shape
    n_experts = router.shape[1]
    router_pad = jnp.pad(router, ((0, 0), (0, LANES - n_experts)))
    h, route = _router(x2, gain, mod3, router_pad, seq, 3, 4, n_experts)

    slots = m * TOP_K
    n_tiles = slots // tm + n_experts
    flat_e = route[:, 0:TOP_K].astype(jnp.int32).reshape(slots)
    onehot = (flat_e[:, None] == jnp.arange(n_experts)[None, :]).astype(jnp.int32)
    rank = jnp.sum((jnp.cumsum(onehot, axis=0) - onehot) * onehot, axis=1)
    counts = jnp.sum(onehot, axis=0)
    padded = (counts + tm - 1) // tm * tm
    ends = jnp.cumsum(padded)
    dest = jnp.sum(onehot * (ends - padded)[None, :], axis=1) + rank
    sorted_tok = (jnp.arange(n_tiles * tm, dtype=jnp.int32) % m).at[dest].set(
        jnp.arange(slots, dtype=jnp.int32) // TOP_K)
    tile_start = jnp.arange(n_tiles, dtype=jnp.int32) * tm
    tile_expert = jnp.minimum(jnp.sum((tile_start[:, None] >= ends[None, :]).astype(jnp.int32), axis=1),
                              n_experts - 1)
    tile_rows = jnp.clip((ends - padded + counts)[tile_expert] - tile_start, 0, tm).astype(jnp.int32)
    tile_new = jnp.concatenate([jnp.ones((1,), jnp.int32),
                                (tile_expert[1:] != tile_expert[:-1]).astype(jnp.int32)])
    tile_group = jnp.cumsum(tile_new) - 1
    tile_idx = jnp.arange(n_tiles, dtype=jnp.int32)
    later_start = (tile_idx[None, :] > tile_idx[:, None]) & (tile_new[None, :] > 0)
    next_start = jnp.min(jnp.where(later_start, tile_idx[None, :], n_tiles), axis=1)
    next_expert = jnp.where(next_start < n_tiles, tile_expert[jnp.minimum(next_start, n_tiles - 1)],
                            tile_expert[0])
    tiles = (tile_expert, tile_rows, tile_new, tile_group.astype(jnp.int32),
             next_expert.astype(jnp.int32), (tile_group[-1:] + 1).astype(jnp.int32))

    xs = _to_bf16(h.at[sorted_tok].get(mode="promise_in_bounds"))
    act = _moe_up(tiles, xs, wg, wu, tm)
    ys = _moe_down(tiles, act, wd, tm)
    dest_slot_major = dest.reshape(m, TOP_K).T.reshape(slots)
    yab = ys.at[dest_slot_major].get(mode="promise_in_bounds")
    return _moe_combine(x2, yab, route, mod3, seq, 5, final_gain)


def _final_norm_kernel(x_ref, g_ref, o_ref):
    x = x_ref[...]
    o_ref[...] = x * lax.rsqrt(jnp.mean(x * x, axis=-1, keepdims=True) + RMS_EPS) * g_ref[...]


def _final_norm(x2, gain, tm=512):
    m, d = x2.shape
    return pl.pallas_call(
        _final_norm_kernel,
        grid=(m // tm,),
        in_specs=[pl.BlockSpec((tm, d), lambda i: (i, 0)),
                  pl.BlockSpec((1, d), lambda i: (0, 0))],
        out_specs=pl.BlockSpec((tm, d), lambda i: (i, 0)),
        out_shape=jax.ShapeDtypeStruct((m, d), F32),
        compiler_params=_cparams(("arbitrary",)),
        name="final_norm",
    )(x2, gain.reshape(1, d))


def _prep_mixer(lp, C):
    w_in, mu = lp["w_in"], lp["shift_mu"]
    n_r = mu.shape[0]
    assert 3 * C < n_r <= 3 * C + LORA_WINDOW <= RWKV_PROJ_COLS
    out = dict(lp)
    out["w_in_r"] = w_in[:, :RWKV_PROJ_COLS].astype(BF16)
    out["w_in_m"] = w_in[:, n_r:].astype(BF16)
    out["mu_pad"] = jnp.pad(mu, (0, 3 * C + LORA_WINDOW - n_r)).reshape(1, -1)
    off = 0
    for name in ("w2", "a2", "g2") + (("v2",) if "v2" in lp else ()):
        rank = lp[name].shape[0]
        out[name + "_pad"] = jnp.pad(lp[name], ((off, LORA_WINDOW - off - rank), (0, 0)))
        off += rank
    assert 3 * C + off == n_r
    return out


def _layer(x2, mod3, lp, batch, seq, v_first, final_gain):
    C = lp["w0"].shape[0]
    lp = _prep_mixer(lp, C)
    h = _normmod(x2, lp["norm_mix"], mod3, seq, 0, 1)
    y_r, v_first = _rwkv(_in_proj(h, lp["w_in_r"]), batch, seq, lp, v_first)
    y_m = _moba(_in_proj(h, lp["w_in_m"]), batch, seq, 0, lp["moba_gain"])
    x2 = _mm_res([y_r, y_m], lp["w_out"], x2, mod3, seq, 2, tm=1024, tn=1024)
    if "ffn_gate" in lp:
        act = _ffn_up(_normmod(x2, lp["norm_ffn"], mod3, seq, 3, 4), lp["ffn_gate"], lp["ffn_up"])
        x2 = _mm_res([act], lp["ffn_down"], x2, mod3, seq, 5)
        if final_gain is not None:
            x2 = _final_norm(x2, final_gain)
    else:
        x2 = _moe(x2, lp["norm_ffn"], mod3, lp["router"], lp["exp_gate"], lp["exp_up"],
                  lp["exp_down"], seq, final_gain)
    return x2, v_first


def _forward(x, c, layers, norm_out):
    batch, seq, d = x.shape
    x2 = x.reshape(batch * seq, d)
    c_pad = jnp.pad(c, ((0, SUBLANES - batch), (0, 0)))
    v_first = None
    for li, lp in enumerate(layers):
        mod = _adaln(c_pad, lp["mod_w"], lp["mod_b"])[:batch]
        mod3 = mod.reshape(batch * 6, 1, d)
        last = li == len(layers) - 1
        x2, v_first = _layer(x2, mod3, lp, batch, seq, v_first, norm_out if last else None)
    return x2.reshape(batch, seq, d)


def kernel(x, c, l0_mod_w, l0_mod_b, l0_norm_mix, l0_w_in, l0_shift_mu, l0_w0, l0_w2, l0_a0, l0_a2, l0_g2, l0_k_k, l0_k_a, l0_r_k, l0_ln_w, l0_ln_b, l0_moba_gain, l0_w_out, l0_norm_ffn, l0_ffn_gate, l0_ffn_up, l0_ffn_down, l1_mod_w, l1_mod_b, l1_norm_mix, l1_w_in, l1_shift_mu, l1_w0, l1_w2, l1_a0, l1_a2, l1_g2, l1_v0, l1_v2, l1_k_k, l1_k_a, l1_r_k, l1_ln_w, l1_ln_b, l1_moba_gain, l1_w_out, l1_norm_ffn, l1_router, l1_exp_gate, l1_exp_up, l1_exp_down, norm_out):
    layers = (
        dict(mod_w=l0_mod_w, mod_b=l0_mod_b, norm_mix=l0_norm_mix, w_in=l0_w_in, shift_mu=l0_shift_mu,
             w0=l0_w0, w2=l0_w2, a0=l0_a0, a2=l0_a2, g2=l0_g2, k_k=l0_k_k, k_a=l0_k_a, r_k=l0_r_k,
             ln_w=l0_ln_w, ln_b=l0_ln_b, moba_gain=l0_moba_gain, w_out=l0_w_out, norm_ffn=l0_norm_ffn,
             ffn_gate=l0_ffn_gate, ffn_up=l0_ffn_up, ffn_down=l0_ffn_down),
        dict(mod_w=l1_mod_w, mod_b=l1_mod_b, norm_mix=l1_norm_mix, w_in=l1_w_in, shift_mu=l1_shift_mu,
             w0=l1_w0, w2=l1_w2, a0=l1_a0, a2=l1_a2, g2=l1_g2, v0=l1_v0, v2=l1_v2, k_k=l1_k_k,
             k_a=l1_k_a, r_k=l1_r_k, ln_w=l1_ln_w, ln_b=l1_ln_b, moba_gain=l1_moba_gain, w_out=l1_w_out,
             norm_ffn=l1_norm_ffn, router=l1_router, exp_gate=l1_exp_gate, exp_up=l1_exp_up,
             exp_down=l1_exp_down),
    )
    return _forward(x, c, layers, norm_out)
```
